```python
import jax, jax.numpy as jnp
from jax import lax
import numpy as np

D_MODEL = 1024
BATCH = 32
SEQ = 256
DEPTH = 2
DEC_BATCH = 4
DEC_SEQ = 1024
PAST_LEN = 256

GRID_W = 64
N_HEADS = 8
QK_NOPE = 64
QK_ROPE = 32
QK_DIM = QK_NOPE + QK_ROPE
V_DIM = 64
Q_LORA = 384
KV_LORA = 256
ROPE_THETA = 10000.0
Q_BLOCK = 128
LRU_WIDTH = 1024
LRU_BLOCKS = 8
LRU_BW = LRU_WIDTH // LRU_BLOCKS
CONV_W = 4
LRU_C = 8.0
FF_HIDDEN = -(-8 * D_MODEL // (3 * 256)) * 256
N_MOD = 6
EPS = 1e-6
OFF_KV = Q_LORA
OFF_KR = OFF_KV + KV_LORA
OFF_LX = OFF_KR + QK_ROPE
OFF_LG = OFF_LX + LRU_WIDTH
OFF_MG = OFF_LG + LRU_WIDTH
IN_COLS = OFF_MG + 2 * D_MODEL

kernel_name = "hybrid_mla_rglru_diffusion_step"


def rms_norm(x, g):
    xf = x.astype(jnp.float32)
    y = xf * lax.rsqrt(jnp.mean(xf * xf, axis=-1, keepdims=True) + EPS)
    return (y * g.astype(jnp.float32)).astype(x.dtype)


def axial_rope_tables(n_tokens):
    rows = n_tokens // GRID_W
    row = jnp.repeat(jnp.arange(rows, dtype=jnp.float32), GRID_W)
    col = jnp.tile(jnp.arange(GRID_W, dtype=jnp.float32), rows)
    n_freq = QK_ROPE // 4
    inv = ROPE_THETA ** (-jnp.arange(n_freq, dtype=jnp.float32) / n_freq)
    ang = jnp.concatenate([row[:, None] * inv, col[:, None] * inv], axis=-1)
    return jnp.cos(ang), jnp.sin(ang)


def apply_rope(x, cos, sin):
    half = QK_ROPE // 2
    x1 = x[..., :half].astype(jnp.float32)
    x2 = x[..., half:].astype(jnp.float32)
    return jnp.concatenate([x1 * cos - x2 * sin, x1 * sin + x2 * cos], axis=-1).astype(x.dtype)


def attend(q, k, v):
    b, lq, h, dk = q.shape
    nb = lq // Q_BLOCK
    qb = jnp.swapaxes(q.reshape(b, nb, Q_BLOCK, h, dk), 0, 1)
    scale = dk ** -0.5

    def block(qi):
        s = jnp.einsum("bqhd,bkhd->bhqk", qi, k, preferred_element_type=jnp.float32) * scale
        p = jax.nn.softmax(s, axis=-1).astype(v.dtype)
        return jnp.einsum("bhqk,bkhd->bqhd", p, v)

    o = lax.map(block, qb)
    return jnp.swapaxes(o, 0, 1).reshape(b, lq, h * v.shape[-1])


def centred_conv(x, w, b):
    L = x.shape[1]
    xp = jnp.pad(x, ((0, 0), (CONV_W // 2, CONV_W - 1 - CONV_W // 2), (0, 0)))
    out = b
    for t in range(CONV_W):
        out = out + xp[:, t:t + L] * w[t]
    return out


def block_diag(x, w, b):
    bsz, L, _ = x.shape
    y = jnp.einsum("blnd,nde->blne", x.reshape(bsz, L, LRU_BLOCKS, LRU_BW), w)
    return y.reshape(bsz, L, LRU_WIDTH) + b


def rg_lru_direction(x, w_rg, b_rg, w_ig, b_ig, lam, h0, reverse):
    r = jax.nn.sigmoid(block_diag(x, w_rg, b_rg).astype(jnp.float32))
    i = jax.nn.sigmoid(block_diag(x, w_ig, b_ig).astype(jnp.float32))
    log_a = -LRU_C * jax.nn.softplus(-lam.astype(jnp.float32)) * r
    a = jnp.exp(log_a)
    u = jnp.sqrt(-jnp.expm1(2.0 * log_a)) * i * x.astype(jnp.float32)

    def step(h, au):
        a_t, u_t = au
        h = a_t * h + u_t
        return h, h

    h_last, hs = lax.scan(step, h0.astype(jnp.float32),
                          (jnp.swapaxes(a, 0, 1), jnp.swapaxes(u, 0, 1)), reverse=reverse)
    return jnp.swapaxes(hs, 0, 1), h_last


def mixer(h, p, latent, ckv_ctx, krope_ctx, hf0, hb0):
    b, L, _ = h.shape
    proj = h @ p["w_in"]
    q_lat = proj[..., :OFF_KV]
    kv_lat = proj[..., OFF_KV:OFF_KR]
    k_rope = proj[..., OFF_KR:OFF_LX]
    lru_x = proj[..., OFF_LX:OFF_LG]
    lru_gate = proj[..., OFF_LG:OFF_MG]
    merge = proj[..., OFF_MG:]
    q = (rms_norm(q_lat, p["g_q"]) @ p["w_q_up"]).reshape(b, L, N_HEADS, QK_DIM)
    q_nope, q_rope = q[..., :QK_NOPE], q[..., QK_NOPE:]
    ckv = rms_norm(kv_lat, p["g_kv"])
    if latent:
        cos, sin = axial_rope_tables(L)
        q_rope = apply_rope(q_rope, cos[:, None, :], sin[:, None, :])
        k_rope = apply_rope(k_rope, cos, sin)
        ckv_keys = jnp.concatenate([ckv, ckv_ctx.astype(ckv.dtype)], axis=1)
        krope_keys = jnp.concatenate([k_rope, krope_ctx.astype(k_rope.dtype)], axis=1)
    else:
        ckv_keys, krope_keys = ckv, k_rope
    lk = ckv_keys.shape[1]
    kv = (ckv_keys @ p["w_kv_up"]).reshape(b, lk, N_HEADS, QK_NOPE + V_DIM)
    k = jnp.concatenate([kv[..., :QK_NOPE],
                         jnp.broadcast_to(krope_keys[:, :, None, :], (b, lk, N_HEADS, QK_ROPE))], axis=-1)
    v = kv[..., QK_NOPE:]
    q = jnp.concatenate([q_nope, q_rope], axis=-1)
    y_mla = attend(q, k, v) @ p["w_o_mla"]
    xc = centred_conv(lru_x, p["conv_w"], p["conv_b"])
    hf_seq, hf = rg_lru_direction(xc, p["w_rg"][0], p["b_rg"][0], p["w_ig"][0], p["b_ig"][0],
                                  p["lru_lambda"][0], hf0, False)
    hb_seq, hb = rg_lru_direction(xc, p["w_rg"][1], p["b_rg"][1], p["w_ig"][1], p["b_ig"][1],
                                  p["lru_lambda"][1], hb0, True)
    y_lru = ((hf_seq + hb_seq).astype(h.dtype) * jax.nn.gelu(lru_gate)) @ p["w_o_lru"]
    gates = jax.nn.sigmoid(merge.astype(jnp.float32)).astype(h.dtype)
    y = (gates[..., :D_MODEL] * y_mla + gates[..., D_MODEL:] * y_lru) @ p["w_out"]
    return y, ckv, k_rope, hf, hb


def swiglu(h, p):
    gu = h @ p["w_ffn_in"]
    return (jax.nn.silu(gu[..., :FF_HIDDEN]) * gu[..., FF_HIDDEN:]) @ p["w_ffn_out"]


def layer(x, mod, p, latent, ckv_ctx, krope_ctx, hf0, hb0):
    sh1, sc1, g1, sh2, sc2, g2 = jnp.split(mod, N_MOD, axis=-1)
    h = rms_norm(x, p["g_pre_mix"]) * (1.0 + sc1) + sh1
    y, ckv, krope, hf, hb = mixer(h, p, latent, ckv_ctx, krope_ctx, hf0, hb0)
    x = x + g1 * rms_norm(y, p["g_post_mix"])
    h = rms_norm(x, p["g_pre_ffn"]) * (1.0 + sc2) + sh2
    x = x + g2 * rms_norm(swiglu(h, p), p["g_post_ffn"])
    return x, ckv, krope, hf, hb


def setup_inputs(seed: int = 0) -> dict:
    key = jax.random.key(seed)
    ks = jax.random.split(key, 40)

    def nrm(k, shape, scale):
        return jax.random.normal(k, shape, jnp.float32) * scale

    a0 = jax.random.uniform(ks[30], (DEPTH, 2, LRU_WIDTH), jnp.float32, 0.9, 0.999)
    s0 = a0 ** (1.0 / LRU_C)
    lam = jnp.log(s0) - jnp.log1p(-s0)
    return {
        "x_prompt": nrm(ks[0], (BATCH, SEQ, D_MODEL), 1.0),
        "x_sample": nrm(ks[1], (DEC_BATCH, DEC_SEQ, D_MODEL), 1.0),
        "cache_ckv": nrm(ks[2], (DEC_BATCH, DEPTH, PAST_LEN, KV_LORA), 1.0),
        "cache_krope": nrm(ks[3], (DEC_BATCH, DEPTH, PAST_LEN, QK_ROPE), 1.0),
        "state_lru_fwd": nrm(ks[4], (DEC_BATCH, DEPTH, LRU_WIDTH), 0.5),
        "state_lru_bwd": nrm(ks[5], (DEC_BATCH, DEPTH, LRU_WIDTH), 0.5),
        "c": nrm(ks[6], (DEC_BATCH, D_MODEL), 1.0),
        "c_ctx": nrm(ks[7], (D_MODEL,), 1.0),
        "w_mod": nrm(ks[8], (DEPTH, D_MODEL, N_MOD * D_MODEL), 0.5 * D_MODEL ** -0.5),
        "b_mod": nrm(ks[9], (DEPTH, N_MOD * D_MODEL), 0.1),
        "g_pre_mix": 1.0 + nrm(ks[10], (DEPTH, D_MODEL), 0.05),
        "g_post_mix": 1.0 + nrm(ks[11], (DEPTH, D_MODEL), 0.05),
        "g_pre_ffn": 1.0 + nrm(ks[12], (DEPTH, D_MODEL), 0.05),
        "g_post_ffn": 1.0 + nrm(ks[13], (DEPTH, D_MODEL), 0.05),
        "w_in": nrm(ks[14], (DEPTH, D_MODEL, IN_COLS), D_MODEL ** -0.5),
        "g_q": 1.0 + nrm(ks[15], (DEPTH, Q_LORA), 0.05),
        "w_q_up": nrm(ks[16], (DEPTH, Q_LORA, N_HEADS * QK_DIM), Q_LORA ** -0.5),
        "g_kv": 1.0 + nrm(ks[17], (DEPTH, KV_LORA), 0.05),
        "w_kv_up": nrm(ks[18], (DEPTH, KV_LORA, N_HEADS * (QK_NOPE + V_DIM)), KV_LORA ** -0.5),
        "w_o_mla": nrm(ks[19], (DEPTH, N_HEADS * V_DIM, D_MODEL), (N_HEADS * V_DIM) ** -0.5),
        "conv_w": nrm(ks[20], (DEPTH, CONV_W, LRU_WIDTH), CONV_W ** -0.5),
        "conv_b": nrm(ks[21], (DEPTH, LRU_WIDTH), 0.02),
        "w_rg": nrm(ks[22], (DEPTH, 2, LRU_BLOCKS, LRU_BW, LRU_BW), LRU_BW ** -0.5),
        "b_rg": nrm(ks[23], (DEPTH, 2, LRU_WIDTH), 0.1),
        "w_ig": nrm(ks[24], (DEPTH, 2, LRU_BLOCKS, LRU_BW, LRU_BW), LRU_BW ** -0.5),
        "b_ig": nrm(ks[25], (DEPTH, 2, LRU_WIDTH), 0.1),
        "lru_lambda": lam,
        "w_o_lru": nrm(ks[26], (DEPTH, LRU_WIDTH, D_MODEL), LRU_WIDTH ** -0.5),
        "w_out": nrm(ks[27], (DEPTH, D_MODEL, D_MODEL), D_MODEL ** -0.5),
        "w_ffn_in": nrm(ks[28], (DEPTH, D_MODEL, 2 * FF_HIDDEN), D_MODEL ** -0.5),
        "w_ffn_out": nrm(ks[29], (DEPTH, FF_HIDDEN, D_MODEL), FF_HIDDEN ** -0.5),
    }


def reference(x_prompt, x_sample, cache_ckv, cache_krope, state_lru_fwd, state_lru_bwd, c, c_ctx,
              w_mod, b_mod, g_pre_mix, g_post_mix, g_pre_ffn, g_post_ffn, w_in, g_q, w_q_up, g_kv,
              w_kv_up, w_o_mla, conv_w, conv_b, w_rg, b_rg, w_ig, b_ig, lru_lambda, w_o_lru, w_out,
              w_ffn_in, w_ffn_out):
    def params(l):
        return {"g_pre_mix": g_pre_mix[l], "g_post_mix": g_post_mix[l], "g_pre_ffn": g_pre_ffn[l],
                "g_post_ffn": g_post_ffn[l], "w_in": w_in[l], "g_q": g_q[l], "w_q_up": w_q_up[l],
                "g_kv": g_kv[l], "w_kv_up": w_kv_up[l], "w_o_mla": w_o_mla[l], "conv_w": conv_w[l],
                "conv_b": conv_b[l], "w_rg": w_rg[l], "b_rg": b_rg[l], "w_ig": w_ig[l], "b_ig": b_ig[l],
                "lru_lambda": lru_lambda[l], "w_o_lru": w_o_lru[l], "w_out": w_out[l],
                "w_ffn_in": w_ffn_in[l], "w_ffn_out": w_ffn_out[l]}

    x = x_prompt
    bp = x_prompt.shape[0]
    zeros_state = jnp.zeros((bp, LRU_WIDTH), jnp.float32)
    ckvs, kropes, hfs, hbs = [], [], [], []
    for l in range(DEPTH):
        mod_ctx = jax.nn.silu(c_ctx) @ w_mod[l] + b_mod[l]
        x, ckv, krope, hf, hb = layer(x, mod_ctx, params(l), False, None, None, zeros_state, zeros_state)
        ckvs.append(ckv)
        kropes.append(krope)
        hfs.append(hf)
        hbs.append(hb)
    y_prompt = x
    new_ckv = jnp.stack(ckvs, axis=1).astype(x_prompt.dtype)
    new_krope = jnp.stack(kropes, axis=1).astype(x_prompt.dtype)
    new_lru_fwd = jnp.stack(hfs, axis=1).astype(x_prompt.dtype)
    new_lru_bwd = jnp.stack(hbs, axis=1).astype(x_prompt.dtype)

    x = x_sample
    for l in range(DEPTH):
        mod_lat = (jax.nn.silu(c) @ w_mod[l] + b_mod[l])[:, None, :]
        x, _, _, _, _ = layer(x, mod_lat, params(l), True, cache_ckv[:, l], cache_krope[:, l],
                              state_lru_fwd[:, l], state_lru_bwd[:, l])
    y_sample = x
    return (y_prompt, y_sample, new_ckv, new_krope, new_lru_fwd, new_lru_bwd)
```

```python
import functools

import jax
import jax.numpy as jnp
from jax import lax
from jax.experimental import pallas as pl
from jax.experimental.pallas import tpu as pltpu

D_MODEL = 1024
DEPTH = 2
GRID_W = 64
N_HEADS = 8
QK_NOPE = 64
QK_ROPE = 32
QK_DIM = QK_NOPE + QK_ROPE
V_DIM = 64
Q_LORA = 384
KV_LORA = 256
ROPE_THETA = 10000.0
LRU_WIDTH = 1024
LRU_BLOCKS = 8
LRU_BW = LRU_WIDTH // LRU_BLOCKS
CONV_W = 4
LRU_C = 8.0
FF_HIDDEN = 2816
N_MOD = 6
EPS = 1e-6

SUBLANES = 8
LANES = 128
VMEM_LIMIT_BYTES = 48 * 1024 * 1024

C_Q = 0
C_KV = C_Q + Q_LORA
C_LX = C_KV + KV_LORA
C_LG = C_LX + LRU_WIDTH
C_MG = C_LG + LRU_WIDTH
C_KR = C_MG + 2 * D_MODEL
IN_COLS_PADDED = C_KR + LANES
Q_NOPE_ALL = N_HEADS * QK_NOPE
Q_ROPE_ALL = N_HEADS * QK_ROPE
Q_COLS = Q_NOPE_ALL + Q_ROPE_ALL
KV_COLS = N_HEADS * (QK_NOPE + V_DIM)

TOKEN_TILE = 256
FFN_CHUNK = 256
LRU_CH = 256
LRU_ROWS = 512
SCAN_UNROLL = 8
MOD_ROWS = 8
MOD_TILE = 1536

F32 = jnp.float32
BF16 = jnp.bfloat16


def _params(n_grid_dims):
    return pltpu.CompilerParams(dimension_semantics=("arbitrary",) * n_grid_dims,
                                vmem_limit_bytes=VMEM_LIMIT_BYTES)


def _resident(shape):
    zeros = (0,) * len(shape)
    return pl.BlockSpec(shape, lambda *_: zeros, pipeline_mode=pl.Buffered(1))


def _rms(x, g):
    return x * lax.rsqrt(jnp.mean(x * x, axis=-1, keepdims=True) + EPS) * g


def _dot(a, b):
    return jnp.dot(a, b, preferred_element_type=F32)


def _dot_t(a, b):
    return lax.dot_general(a, b, (((1,), (1,)), ((), ())), preferred_element_type=F32)


class _Group:
    def __init__(self, batch, seq_len, mod_row0, mod_row_stride, latent):
        self.batch, self.seq_len, self.latent = batch, seq_len, latent
        self.mod_row0, self.mod_row_stride = mod_row0, mod_row_stride
        self.tokens = batch * seq_len
        self.n_chunks = max(1, SUBLANES // batch)
        assert self.n_chunks in (1, 2) and (batch * self.n_chunks) % SUBLANES == 0
        self.chunk_len = seq_len // self.n_chunks
        self.lru_groups = batch * self.n_chunks // SUBLANES
        self.tiles_per_seq = seq_len // TOKEN_TILE
        self.tiles_per_chunk = self.chunk_len // TOKEN_TILE
        assert self.chunk_len % TOKEN_TILE == 0
        self.n_tiles = self.tokens // TOKEN_TILE

    def mod_spec(self):
        return pl.BlockSpec((None, 1, N_MOD * D_MODEL),
                            lambda i: (self.mod_row0 + (i // self.tiles_per_seq) * self.mod_row_stride, 0, 0))

    def time_major_spec(self):
        def index(i):
            b, j = i // self.tiles_per_seq, i % self.tiles_per_seq
            p = b * self.n_chunks + j // self.tiles_per_chunk
            return (p // SUBLANES, j % self.tiles_per_chunk, p % SUBLANES)
        return pl.BlockSpec((None, TOKEN_TILE, LRU_WIDTH), index)


def _row_spec(width):
    return pl.BlockSpec((TOKEN_TILE, width), lambda i: (i, 0))


def _mod_kernel(c_ref, w_ref, b_ref, o_ref):
    c = c_ref[...]
    a = (c * jax.nn.sigmoid(c)).astype(BF16)
    o_ref[...] = _dot(a, w_ref[...].astype(BF16)) + b_ref[...]


def _modulation(cond, w_mod, b_mod):
    n = N_MOD * D_MODEL
    return pl.pallas_call(
        _mod_kernel,
        grid=(DEPTH, n // MOD_TILE),
        in_specs=[pl.BlockSpec((MOD_ROWS, D_MODEL), lambda l, j: (0, 0)),
                  pl.BlockSpec((None, D_MODEL, MOD_TILE), lambda l, j: (l, 0, j)),
                  pl.BlockSpec((None, 1, MOD_TILE), lambda l, j: (l, 0, j))],
        out_specs=pl.BlockSpec((None, MOD_ROWS, MOD_TILE), lambda l, j: (l, 0, j)),
        out_shape=jax.ShapeDtypeStruct((DEPTH, MOD_ROWS, n), F32),
        compiler_params=_params(2),
    )(cond, w_mod, b_mod.reshape(DEPTH, 1, n))


def _in_proj_kernel(latent, x_ref, mod_ref, gpre_ref, win_ref, gq_ref, wq_ref, gkv_ref, wkv_ref, *rest):
    if latent:
        cosq_ref, sinq_ref, ktab_ref = rest[:3]
        rest = rest[3:]
    q_ref, ckv_ref, kv_ref, kr_ref, lx_ref, gg_ref, gates_ref = rest
    mod = mod_ref[...]
    sh1, sc1 = mod[:, 0:D_MODEL], mod[:, D_MODEL:2 * D_MODEL]
    hb = (_rms(x_ref[...], gpre_ref[...]) * (1.0 + sc1) + sh1).astype(BF16)

    def proj(lo, hi):
        return _dot(hb, win_ref[:, lo:hi])

    qn = _rms(proj(C_Q, C_KV), gq_ref[...]).astype(BF16)
    qf = _dot(qn, wq_ref[...])
    if latent:
        q_ref[:, :Q_NOPE_ALL] = qf[:, :Q_NOPE_ALL].astype(BF16)
        rot = qf[:, Q_NOPE_ALL:Q_COLS] * cosq_ref[...] + qf[:, Q_COLS:] * sinq_ref[...]
        q_ref[:, Q_NOPE_ALL:] = rot.astype(BF16)
    else:
        q_ref[...] = qf.astype(BF16)

    ckv = _rms(proj(C_KV, C_LX), gkv_ref[...])
    ckv_ref[...] = ckv
    kv_ref[...] = _dot(ckv.astype(BF16), wkv_ref[...]).astype(BF16)

    kr = proj(C_KR, IN_COLS_PADDED)
    if latent:
        pr = kr * ktab_ref[...]
        kr = pr + pltpu.roll(pr, LANES - QK_ROPE, 1)
    kr_ref[...] = kr[:, :QK_ROPE]

    lx_ref[...] = proj(C_LX, C_LG)
    gg_ref[...] = jax.nn.gelu(proj(C_LG, C_MG))
    gates_ref[...] = jax.nn.sigmoid(proj(C_MG, C_KR))


def _in_proj(grp, x, mod, g_pre, w_in, g_q, w_q, g_kv, w_kv, rope_tabs):
    t = grp.tokens
    tm_shape = (grp.lru_groups, grp.chunk_len, SUBLANES * LRU_WIDTH)
    in_specs = [_row_spec(D_MODEL), grp.mod_spec(), _resident((1, D_MODEL)),
                _resident(w_in.shape), _resident((1, Q_LORA)), _resident(w_q.shape),
                _resident((1, KV_LORA)), _resident(w_kv.shape)]
    args = [x, mod, g_pre, w_in, g_q, w_q, g_kv, w_kv]
    if grp.latent:
        per_seq = lambda width: pl.BlockSpec((TOKEN_TILE, width), lambda i: (i % grp.tiles_per_seq, 0))
        in_specs += [per_seq(Q_ROPE_ALL), per_seq(Q_ROPE_ALL), per_seq(LANES)]
        args += list(rope_tabs)
    return pl.pallas_call(
        functools.partial(_in_proj_kernel, grp.latent),
        grid=(grp.n_tiles,),
        in_specs=in_specs,
        out_specs=[_row_spec(Q_COLS), _row_spec(KV_LORA), _row_spec(KV_COLS), _row_spec(QK_ROPE),
                   grp.time_major_spec(), grp.time_major_spec(), _row_spec(2 * D_MODEL)],
        out_shape=[jax.ShapeDtypeStruct((t, Q_COLS), BF16), jax.ShapeDtypeStruct((t, KV_LORA), F32),
                   jax.ShapeDtypeStruct((t, KV_COLS), BF16), jax.ShapeDtypeStruct((t, QK_ROPE), F32),
                   jax.ShapeDtypeStruct(tm_shape, F32), jax.ShapeDtypeStruct(tm_shape, F32),
                   jax.ShapeDtypeStruct((t, 2 * D_MODEL), F32)],
        compiler_params=_params(1),
    )(*args)


def _kv_up_kernel(c_ref, w_ref, o_ref):
    o_ref[...] = _dot(c_ref[...].astype(BF16), w_ref[...]).astype(BF16)


def _kv_up(ckv, w_kv, rows_per_step):
    n = ckv.shape[0]
    return pl.pallas_call(
        _kv_up_kernel,
        grid=(n // rows_per_step,),
        in_specs=[pl.BlockSpec((rows_per_step, KV_LORA), lambda i: (i, 0)), _resident(w_kv.shape)],
        out_specs=pl.BlockSpec((rows_per_step, KV_COLS), lambda i: (i, 0)),
        out_shape=jax.ShapeDtypeStruct((n, KV_COLS), BF16),
        compiler_params=_params(1),
    )(ckv, w_kv)


def _attn_kernel(with_ctx, q_ref, kv_ref, kr_ref, *rest):
    if with_ctx:
        kvc_ref, krc_ref, o_ref = rest
    else:
        (o_ref,) = rest
    scale = QK_DIM ** -0.5
    kr = kr_ref[...].astype(BF16)
    if with_ctx:
        krc = krc_ref[...].astype(BF16)
    for h in range(N_HEADS):
        nope = slice(h * QK_NOPE, (h + 1) * QK_NOPE)
        rope = slice(Q_NOPE_ALL + h * QK_ROPE, Q_NOPE_ALL + (h + 1) * QK_ROPE)
        val = slice(Q_NOPE_ALL + h * V_DIM, Q_NOPE_ALL + (h + 1) * V_DIM)
        qn, qr = q_ref[:, nope], q_ref[:, rope]
        s = (_dot_t(qn, kv_ref[:, nope]) + _dot_t(qr, kr)) * scale
        m = jnp.max(s, axis=-1, keepdims=True)
        if with_ctx:
            sc = (_dot_t(qn, kvc_ref[:, nope]) + _dot_t(qr, krc)) * scale
            m = jnp.maximum(m, jnp.max(sc, axis=-1, keepdims=True))
        p = jnp.exp(s - m)
        den = jnp.sum(p, axis=-1, keepdims=True)
        o = _dot(p.astype(BF16), kv_ref[:, val])
        if with_ctx:
            pc = jnp.exp(sc - m)
            den = den + jnp.sum(pc, axis=-1, keepdims=True)
            o = o + _dot(pc.astype(BF16), kvc_ref[:, val])
        o_ref[:, h * V_DIM:(h + 1) * V_DIM] = (o / den).astype(BF16)


def _attention(grp, q, kv, kr, kv_ctx=None, kr_ctx=None):
    with_ctx = kv_ctx is not None
    nq = grp.tiles_per_seq
    seq = lambda rows, width: pl.BlockSpec((rows, width), lambda b, i: (b, 0))
    in_specs = [pl.BlockSpec((TOKEN_TILE, Q_COLS), lambda b, i: (b * nq + i, 0)),
                seq(grp.seq_len, KV_COLS), seq(grp.seq_len, QK_ROPE)]
    args = [q, kv, kr]
    if with_ctx:
        past = kv_ctx.shape[0] // grp.batch
        in_specs += [seq(past, KV_COLS), seq(past, QK_ROPE)]
        args += [kv_ctx, kr_ctx]
    return pl.pallas_call(
        functools.partial(_attn_kernel, with_ctx),
        grid=(grp.batch, nq),
        in_specs=in_specs,
        out_specs=pl.BlockSpec((TOKEN_TILE, N_HEADS * V_DIM), lambda b, i: (b * nq + i, 0)),
        out_shape=jax.ShapeDtypeStruct((grp.tokens, N_HEADS * V_DIM), BF16),
        compiler_params=_params(2),
    )(*args)


def _softplus(z):
    return jnp.maximum(z, 0.0) + jnp.log1p(jnp.exp(-jnp.abs(z)))


def _lru_kernel(n_chunks, chunk_len, with_state, x_ref, gg_ref, cw_ref, cb_ref, wrg_ref, brg_ref,
                wig_ref, big_ref, lam_ref, hf0_ref, hb0_ref, *rest):
    if with_state:
        s_ref, hf_ref, hb_ref, xpad, a_f, u_f, a_b, u_b = rest
    else:
        s_ref, xpad, a_f, u_f, a_b, u_b = rest
    rows = chunk_len * SUBLANES
    halo = (CONV_W // 2) * SUBLANES
    sub = lax.broadcasted_iota(jnp.int32, (SUBLANES, LRU_CH), 0)
    first_chunk = (sub % n_chunks) == 0
    last_chunk = (sub % n_chunks) == n_chunks - 1

    def from_prev(v):
        return jnp.where(first_chunk, 0.0, pltpu.roll(v, 1, 0))

    def from_next(v):
        return jnp.where(last_chunk, 0.0, pltpu.roll(v, SUBLANES - 1, 0))

    xpad[halo:halo + rows, :] = x_ref[...]
    xpad[0:SUBLANES, :] = from_prev(x_ref[rows - 2 * SUBLANES:rows - SUBLANES, :])
    xpad[SUBLANES:halo, :] = from_prev(x_ref[rows - SUBLANES:rows, :])
    xpad[halo + rows:halo + rows + SUBLANES, :] = from_next(x_ref[0:SUBLANES, :])

    def gates(c, carry):
        r0 = pl.multiple_of(c * LRU_ROWS, LRU_ROWS)
        xc = cb_ref[...]
        for k in range(CONV_W):
            xc = xc + xpad[pl.ds(r0 + k * SUBLANES, LRU_ROWS), :] * cw_ref[k:k + 1, :]
        xcb = xc.astype(BF16)
        for d, (a_ref, u_ref) in enumerate(((a_f, u_f), (a_b, u_b))):
            for n in range(LRU_CH // LRU_BW):
                sl = slice(n * LRU_BW, (n + 1) * LRU_BW)
                r = jax.nn.sigmoid(_dot(xcb[:, sl], wrg_ref[d, n]) + brg_ref[d:d + 1, sl])
                i = jax.nn.sigmoid(_dot(xcb[:, sl], wig_ref[d, n]) + big_ref[d:d + 1, sl])
                log_a = (-LRU_C * _softplus(-lam_ref[d:d + 1, sl])) * r
                a = jnp.exp(log_a)
                a_ref[pl.ds(r0, LRU_ROWS), sl] = a
                one_minus_a2 = -jnp.tanh(log_a) * (a * a + 1.0)
                u_ref[pl.ds(r0, LRU_ROWS), sl] = jnp.sqrt(one_minus_a2) * i * xc[:, sl]
        return carry

    lax.fori_loop(0, rows // LRU_ROWS, gates, 0)

    def scan(hf, hb, store):
        def step(t, carry):
            hf, hb = carry
            rf = pl.multiple_of(t * SUBLANES, SUBLANES)
            rb = pl.multiple_of((chunk_len - 1 - t) * SUBLANES, SUBLANES)
            hf = a_f[pl.ds(rf, SUBLANES), :] * hf + u_f[pl.ds(rf, SUBLANES), :]
            hb = a_b[pl.ds(rb, SUBLANES), :] * hb + u_b[pl.ds(rb, SUBLANES), :]
            if store:
                u_f[pl.ds(rf, SUBLANES), :] = hf
                u_b[pl.ds(rb, SUBLANES), :] = hb
            return hf, hb
        return lax.fori_loop(0, chunk_len, step, (hf, hb), unroll=SCAN_UNROLL)

    hf, hb = hf0_ref[...], hb0_ref[...]
    if n_chunks > 1:
        ef, eb = scan(hf, hb, False)
        hf = jnp.where(first_chunk, hf, pltpu.roll(ef, 1, 0))
        hb = jnp.where(last_chunk, hb, pltpu.roll(eb, SUBLANES - 1, 0))
    hf, hb = scan(hf, hb, True)
    if with_state:
        hf_ref[...] = hf
        hb_ref[...] = hb

    def combine(c, carry):
        r = pl.ds(pl.multiple_of(c * LRU_ROWS, LRU_ROWS), LRU_ROWS)
        s_ref[r, :] = ((u_f[r, :] + u_b[r, :]) * gg_ref[r, :]).astype(BF16)
        return carry

    lax.fori_loop(0, rows // LRU_ROWS, combine, 0)


def _rg_lru(grp, lx, gg, conv_w, conv_b, w_rg, b_rg, w_ig, b_ig, lam, hf0, hb0, with_state):
    assert n_chunks_ok(grp, with_state)
    rows = grp.chunk_len * SUBLANES
    nb = LRU_CH // LRU_BW
    tile = pl.BlockSpec((rows, LRU_CH), lambda g, c: (g, c))
    chan = lambda r: pl.BlockSpec((r, LRU_CH), lambda g, c: (0, c))
    wblk = pl.BlockSpec((2, nb, LRU_BW, LRU_BW), lambda g, c: (0, c, 0, 0))
    state = pl.BlockSpec((SUBLANES, LRU_CH), lambda g, c: (g, c))
    n_seq = grp.lru_groups * SUBLANES
    out_specs = [tile]
    out_shape = [jax.ShapeDtypeStruct((grp.lru_groups * rows, LRU_WIDTH), BF16)]
    if with_state:
        out_specs += [state, state]
        out_shape += [jax.ShapeDtypeStruct((n_seq, LRU_WIDTH), F32)] * 2
    scratch = [pltpu.VMEM((rows + (CONV_W - 1) * SUBLANES, LRU_CH), F32)] + [pltpu.VMEM((rows, LRU_CH), F32)] * 4
    return pl.pallas_call(
        functools.partial(_lru_kernel, grp.n_chunks, grp.chunk_len, with_state),
        grid=(grp.lru_groups, LRU_WIDTH // LRU_CH),
        in_specs=[tile, tile, chan(CONV_W), chan(1), wblk, chan(2), wblk, chan(2), chan(2), state, state],
        out_specs=out_specs,
        out_shape=out_shape,
        scratch_shapes=scratch,
        compiler_params=_params(2),
    )(lx, gg, conv_w, conv_b, w_rg, b_rg, w_ig, b_ig, lam, hf0, hb0)


def n_chunks_ok(grp, with_state):
    return grp.n_chunks == 1 or not with_state


def _merge_kernel(o_ref, s_ref, gates_ref, x_ref, mod_ref, gpost_ref, wom_ref, wol_ref, wout_ref, y_ref):
    y_mla = _dot(o_ref[...], wom_ref[...])
    y_lru = _dot(s_ref[...], wol_ref[...])
    z = gates_ref[:, :D_MODEL] * y_mla + gates_ref[:, D_MODEL:] * y_lru
    y = _dot(z.astype(BF16), wout_ref[...])
    g1 = mod_ref[...][:, 2 * D_MODEL:3 * D_MODEL]
    y_ref[...] = x_ref[...] + g1 * _rms(y, gpost_ref[...])


def _merge(grp, o, s, gates, x, mod, g_post, w_o_mla, w_o_lru, w_out):
    return pl.pallas_call(
        _merge_kernel,
        grid=(grp.n_tiles,),
        in_specs=[_row_spec(N_HEADS * V_DIM), grp.time_major_spec(), _row_spec(2 * D_MODEL), _row_spec(D_MODEL),
                  grp.mod_spec(), _resident((1, D_MODEL)), _resident(w_o_mla.shape), _resident(w_o_lru.shape),
                  _resident(w_out.shape)],
        out_specs=_row_spec(D_MODEL),
        out_shape=jax.ShapeDtypeStruct((grp.tokens, D_MODEL), F32),
        compiler_params=_params(1),
    )(o, s, gates, x, mod, g_post, w_o_mla, w_o_lru, w_out)


def _ffn_kernel(x_ref, mod_ref, gpre_ref, gpost_ref, w1_ref, w2_ref, y_ref):
    x = x_ref[...]
    mod = mod_ref[...]
    sh2, sc2, g2 = (mod[:, k * D_MODEL:(k + 1) * D_MODEL] for k in (3, 4, 5))
    hb = (_rms(x, gpre_ref[...]) * (1.0 + sc2) + sh2).astype(BF16)
    acc = jnp.zeros((TOKEN_TILE, D_MODEL), F32)
    for c in range(FF_HIDDEN // FFN_CHUNK):
        lo = c * FFN_CHUNK
        g = _dot(hb, w1_ref[:, lo:lo + FFN_CHUNK])
        u = _dot(hb, w1_ref[:, FF_HIDDEN + lo:FF_HIDDEN + lo + FFN_CHUNK])
        act = (g * jax.nn.sigmoid(g) * u).astype(BF16)
        acc = acc + _dot(act, w2_ref[lo:lo + FFN_CHUNK, :])
    y_ref[...] = x + g2 * _rms(acc, gpost_ref[...])


def _ffn(grp, x, mod, g_pre, g_post, w1, w2):
    return pl.pallas_call(
        _ffn_kernel,
        grid=(grp.n_tiles,),
        in_specs=[_row_spec(D_MODEL), grp.mod_spec(), _resident((1, D_MODEL)), _resident((1, D_MODEL)),
                  _resident(w1.shape), _resident(w2.shape)],
        out_specs=_row_spec(D_MODEL),
        out_shape=jax.ShapeDtypeStruct((grp.tokens, D_MODEL), F32),
        compiler_params=_params(1),
    )(x, mod, g_pre, g_post, w1, w2)


def _rot_partner(w):
    half = QK_ROPE // 2
    return jnp.concatenate([-w[..., half:], w[..., :half]], axis=-1)


def _rope_tables(n_tokens):
    rows = n_tokens // GRID_W
    row = jnp.repeat(jnp.arange(rows, dtype=F32), GRID_W)
    col = jnp.tile(jnp.arange(GRID_W, dtype=F32), rows)
    n_freq = QK_ROPE // 4
    inv = ROPE_THETA ** (-jnp.arange(n_freq, dtype=F32) / n_freq)
    ang = jnp.concatenate([row[:, None] * inv, col[:, None] * inv], axis=-1)
    cos, sin = jnp.cos(ang), jnp.sin(ang)
    cos2, sin2 = jnp.concatenate([cos, cos], -1), jnp.concatenate([sin, sin], -1)
    ktab = jnp.concatenate([cos2, sin2, jnp.zeros((n_tokens, LANES - 2 * QK_ROPE), F32)], -1)
    return jnp.tile(cos2, (1, N_HEADS)), jnp.tile(sin2, (1, N_HEADS)), ktab


def kernel(x_prompt, x_sample, cache_ckv, cache_krope, state_lru_fwd, state_lru_bwd, c, c_ctx, w_mod, b_mod, g_pre_mix, g_post_mix, g_pre_ffn, g_post_ffn, w_in, g_q, w_q_up, g_kv, w_kv_up, w_o_mla, conv_w, conv_b, w_rg, b_rg, w_ig, b_ig, lru_lambda, w_o_lru, w_out, w_ffn_in, w_ffn_out):
    batch, seq = x_prompt.shape[:2]
    dec_batch, dec_seq = x_sample.shape[:2]
    past = cache_ckv.shape[2]
    assert dec_batch + 1 <= MOD_ROWS

    off_kr = Q_LORA + KV_LORA
    w_kr = w_in[:, :, off_kr:off_kr + QK_ROPE]
    w_in_r = jnp.concatenate(
        [w_in[:, :, :off_kr], w_in[:, :, off_kr + QK_ROPE:], w_kr, _rot_partner(w_kr),
         jnp.zeros((DEPTH, D_MODEL, LANES - 2 * QK_ROPE), F32)], axis=-1).astype(BF16)
    wq = w_q_up.reshape(DEPTH, Q_LORA, N_HEADS, QK_DIM)
    wq_rope = wq[..., QK_NOPE:]
    w_q_r = jnp.concatenate([wq[..., :QK_NOPE].reshape(DEPTH, Q_LORA, Q_NOPE_ALL),
                             wq_rope.reshape(DEPTH, Q_LORA, Q_ROPE_ALL),
                             _rot_partner(wq_rope).reshape(DEPTH, Q_LORA, Q_ROPE_ALL)], axis=-1).astype(BF16)
    wkv = w_kv_up.reshape(DEPTH, KV_LORA, N_HEADS, QK_NOPE + V_DIM)
    w_kv_r = jnp.concatenate([wkv[..., :QK_NOPE].reshape(DEPTH, KV_LORA, Q_NOPE_ALL),
                              wkv[..., QK_NOPE:].reshape(DEPTH, KV_LORA, N_HEADS * V_DIM)], axis=-1).astype(BF16)
    w_o_mla_b, w_o_lru_b, w_out_b = w_o_mla.astype(BF16), w_o_lru.astype(BF16), w_out.astype(BF16)
    w_rg_b, w_ig_b = w_rg.astype(BF16), w_ig.astype(BF16)
    w1_b, w2_b = w_ffn_in.astype(BF16), w_ffn_out.astype(BF16)
    row = lambda v: v.reshape(DEPTH, 1, -1)
    g_pre_mix, g_post_mix, g_pre_ffn, g_post_ffn = map(row, (g_pre_mix, g_post_mix, g_pre_ffn, g_post_ffn))
    g_q, g_kv, conv_b = row(g_q), row(g_kv), row(conv_b)

    cond = jnp.concatenate([c_ctx[None, :], c, jnp.zeros((MOD_ROWS - 1 - dec_batch, D_MODEL), F32)], axis=0)
    mod_all = _modulation(cond, w_mod, b_mod).reshape(DEPTH, MOD_ROWS, 1, N_MOD * D_MODEL)

    def run_group(grp, x, layer_inputs):
        x = x.reshape(grp.tokens, D_MODEL)
        rope_tabs = _rope_tables(grp.seq_len) if grp.latent else None
        tm_rows = grp.lru_groups * grp.chunk_len * SUBLANES
        tm_wide = (grp.lru_groups, grp.chunk_len, SUBLANES * LRU_WIDTH)
        per_layer = []
        for l in range(DEPTH):
            mod = mod_all[l]
            w_q_l = w_q_r[l] if grp.latent else w_q_r[l, :, :Q_COLS]
            q, ckv, kv, kr, lx, gg, gates = _in_proj(grp, x, mod, g_pre_mix[l], w_in_r[l], g_q[l], w_q_l,
                                                     g_kv[l], w_kv_r[l], rope_tabs)
            hf0, hb0, kv_ctx, kr_ctx = layer_inputs(l, w_kv_r[l])
            o = _attention(grp, q, kv, kr, kv_ctx, kr_ctx)
            lru = _rg_lru(grp, lx.reshape(tm_rows, LRU_WIDTH), gg.reshape(tm_rows, LRU_WIDTH), conv_w[l], conv_b[l],
                          w_rg_b[l], b_rg[l], w_ig_b[l], b_ig[l], lru_lambda[l], hf0, hb0, not grp.latent)
            x = _merge(grp, o, lru[0].reshape(tm_wide), gates, x, mod, g_post_mix[l], w_o_mla_b[l], w_o_lru_b[l],
                       w_out_b[l])
            x = _ffn(grp, x, mod, g_pre_ffn[l], g_post_ffn[l], w1_b[l], w2_b[l])
            per_layer.append((ckv, kr) + tuple(lru[1:]))
        return x, per_layer

    ctx = _Group(batch, seq, 0, 0, latent=False)
    zeros_state = jnp.zeros((batch, LRU_WIDTH), F32)
    y_prompt, ctx_layers = run_group(ctx, x_prompt, lambda l, w: (zeros_state, zeros_state, None, None))
    stack = lambda k, shape: jnp.stack([lay[k].reshape(shape) for lay in ctx_layers], axis=1)
    new_ckv = stack(0, (batch, seq, KV_LORA))
    new_krope = stack(1, (batch, seq, QK_ROPE))
    new_lru_fwd = stack(2, (batch, LRU_WIDTH))
    new_lru_bwd = stack(3, (batch, LRU_WIDTH))

    lat = _Group(dec_batch, dec_seq, 1, 1, latent=True)

    def latent_inputs(l, w_kv_l):
        per_chunk = lambda s: jnp.repeat(s[:, l], lat.n_chunks, axis=0)
        kv_ctx = _kv_up(cache_ckv[:, l].reshape(dec_batch * past, KV_LORA), w_kv_l, past)
        return (per_chunk(state_lru_fwd), per_chunk(state_lru_bwd), kv_ctx,
                cache_krope[:, l].reshape(dec_batch * past, QK_ROPE))

    y_sample, _ = run_group(lat, x_sample, latent_inputs)
    return (y_prompt.reshape(batch, seq, D_MODEL), y_sample.reshape(dec_batch, dec_seq, D_MODEL),
            new_ckv, new_krope, new_lru_fwd, new_lru_bwd)
```

```python
import functools

import numpy as np

import jax
import jax.numpy as jnp
from jax import lax
from jax.experimental import pallas as pl
from jax.experimental.pallas import tpu as pltpu

D_MODEL = 1024
DEPTH = 2
GRID_W = 64
N_HEADS = 8
QK_NOPE = 64
QK_ROPE = 32
QK_DIM = QK_NOPE + QK_ROPE
V_DIM = 64
Q_LORA = 384
KV_LORA = 256
ROPE_THETA = 10000.0
LRU_WIDTH = 1024
LRU_BLOCKS = 8
LRU_BW = LRU_WIDTH // LRU_BLOCKS
CONV_W = 4
LRU_C = 8.0
FF_HIDDEN = 2816
N_MOD = 6
EPS = 1e-6

SUBLANES = 8
LANES = 128
VMEM_LIMIT_BYTES = 48 * 1024 * 1024

C_Q = 0
C_KV = C_Q + Q_LORA
C_LX = C_KV + KV_LORA
C_LG = C_LX + LRU_WIDTH
C_MG = C_LG + LRU_WIDTH
C_KR = C_MG + 2 * D_MODEL
IN_COLS_PADDED = C_KR + LANES
Q_NOPE_ALL = N_HEADS * QK_NOPE
Q_ROPE_ALL = N_HEADS * QK_ROPE
Q_COLS = Q_NOPE_ALL + Q_ROPE_ALL
KV_COLS = N_HEADS * (QK_NOPE + V_DIM)
ATTN_OUT = N_HEADS * V_DIM

STEPS = 32
TILE_ROWS = SUBLANES * STEPS
Q_BLOCK = 256
FFN_CHUNK = 256
LRU_CH = 256
LRU_ROWS = 512
SCAN_UNROLL = 8
MOD_TILE = 1536

F32 = jnp.float32
BF16 = jnp.bfloat16


def _params(n_grid_dims):
    return pltpu.CompilerParams(dimension_semantics=("arbitrary",) * n_grid_dims,
                                vmem_limit_bytes=VMEM_LIMIT_BYTES)


def _resident(shape):
    zeros = (0,) * len(shape)
    return pl.BlockSpec(shape, lambda *_: zeros, pipeline_mode=pl.Buffered(1))


def _rms(x, g):
    return x * lax.rsqrt(jnp.mean(x * x, axis=-1, keepdims=True) + EPS) * g


def _dot(a, b):
    return jnp.dot(a, b, preferred_element_type=F32)


def _dot_t(a, b):
    return lax.dot_general(a, b, (((1,), (1,)), ((), ())), preferred_element_type=F32)


def _flat(x3):
    return x3.reshape(TILE_ROWS, x3.shape[-1])


def _tiled(x2):
    return x2.reshape(SUBLANES, STEPS, x2.shape[-1])


def _to_time_major():
    r = np.arange(TILE_ROWS)
    perm = np.zeros((TILE_ROWS, TILE_ROWS), np.float32)
    perm[r, (r % SUBLANES) * STEPS + r // SUBLANES] = 1.0
    return perm


class _Group:
    def __init__(self, batch, seq_len, latent):
        self.batch, self.seq_len, self.latent = batch, seq_len, latent
        self.tokens = batch * seq_len
        self.n_chunks = max(1, SUBLANES // batch)
        self.n_pseudo = batch * self.n_chunks
        assert self.n_chunks in (1, 2) and self.n_pseudo % SUBLANES == 0
        self.chunk_len = seq_len // self.n_chunks
        assert self.chunk_len % STEPS == 0 and seq_len % Q_BLOCK == 0
        self.lru_groups = self.n_pseudo // SUBLANES
        self.tiles_per_chunk = self.chunk_len // STEPS
        self.grid = (self.lru_groups, self.tiles_per_chunk)

    def seq_major(self, width):
        return pl.BlockSpec((SUBLANES, STEPS, width), lambda g, i: (g, i, 0))

    def time_major(self, width):
        return pl.BlockSpec((TILE_ROWS, width), lambda g, i: (g * self.tiles_per_chunk + i, 0))

    def per_pseudo(self, width):
        return pl.BlockSpec((SUBLANES, 1, width), lambda g, i: (g, 0, 0))

    def by_position(self, width):
        return pl.BlockSpec((SUBLANES, STEPS, width), lambda g, i: (0, i, 0))

    def shape3(self, width, dtype):
        return jax.ShapeDtypeStruct((self.n_pseudo, self.chunk_len, width), dtype)

    def shape_tm(self, width, dtype):
        return jax.ShapeDtypeStruct((self.lru_groups * self.chunk_len * SUBLANES, width), dtype)


def _mod_kernel(c_ref, w_ref, b_ref, o_ref):
    c = c_ref[...]
    a = (c * jax.nn.sigmoid(c)).astype(BF16)
    o_ref[...] = _dot(a, w_ref[...].astype(BF16)) + b_ref[...]


def _modulation(cond, w_mod, b_mod):
    n = N_MOD * D_MODEL
    rows = cond.shape[0]
    return pl.pallas_call(
        _mod_kernel,
        grid=(DEPTH, n // MOD_TILE),
        in_specs=[pl.BlockSpec((rows, D_MODEL), lambda l, j: (0, 0)),
                  pl.BlockSpec((None, D_MODEL, MOD_TILE), lambda l, j: (l, 0, j)),
                  pl.BlockSpec((None, 1, MOD_TILE), lambda l, j: (l, 0, j))],
        out_specs=pl.BlockSpec((None, rows, MOD_TILE), lambda l, j: (l, 0, j)),
        out_shape=jax.ShapeDtypeStruct((DEPTH, rows, n), F32),
        compiler_params=_params(2),
        name="modulation",
    )(cond, w_mod, b_mod.reshape(DEPTH, 1, n))


def _in_proj_kernel(latent, x_ref, mod_ref, perm_ref, gpre_ref, win_ref, gq_ref, wq_ref, gkv_ref, wkv_ref, *rest):
    if latent:
        cosq_ref, sinq_ref, ktab_ref = rest[:3]
        rest = rest[3:]
    q_ref, ckv_ref, kv_ref, kr_ref, lx_ref, gg_ref, gates_ref = rest
    sh1, sc1 = mod_ref[:, :, 0:D_MODEL], mod_ref[:, :, D_MODEL:2 * D_MODEL]
    hb = _flat((_rms(x_ref[...], gpre_ref[...]) * (1.0 + sc1) + sh1).astype(BF16))

    def proj(lhs, lo, hi):
        return _dot(lhs, win_ref[:, lo:hi])

    qn = _rms(proj(hb, C_Q, C_KV), gq_ref[...]).astype(BF16)
    qf = _dot(qn, wq_ref[...])
    if latent:
        q_ref[:, :, :Q_NOPE_ALL] = _tiled(qf[:, :Q_NOPE_ALL].astype(BF16))
        rot = qf[:, Q_NOPE_ALL:Q_COLS] * _flat(cosq_ref[...]) + qf[:, Q_COLS:] * _flat(sinq_ref[...])
        q_ref[:, :, Q_NOPE_ALL:] = _tiled(rot.astype(BF16))
    else:
        q_ref[...] = _tiled(qf.astype(BF16))

    ckv = _rms(proj(hb, C_KV, C_LX), gkv_ref[...])
    ckv_ref[...] = _tiled(ckv)
    kv_ref[...] = _tiled(_dot(ckv.astype(BF16), wkv_ref[...]).astype(BF16))

    kr = proj(hb, C_KR, IN_COLS_PADDED)
    if latent:
        pr = kr * _flat(ktab_ref[...])
        kr = pr + pltpu.roll(pr, LANES - QK_ROPE, 1)
    kr_ref[...] = _tiled(kr[:, :QK_ROPE])

    gates_ref[...] = _tiled(jax.nn.sigmoid(proj(hb, C_MG, C_KR)))

    hb_tm = _dot(perm_ref[...], hb).astype(BF16)
    lx_ref[...] = proj(hb_tm, C_LX, C_LG)
    gg_ref[...] = jax.nn.gelu(proj(hb_tm, C_LG, C_MG))


def _in_proj(grp, x, mod, perm, g_pre, w_in, g_q, w_q, g_kv, w_kv, rope_tabs):
    in_specs = [grp.seq_major(D_MODEL), grp.per_pseudo(N_MOD * D_MODEL), _resident(perm.shape),
                _resident((1, D_MODEL)), _resident(w_in.shape), _resident((1, Q_LORA)), _resident(w_q.shape),
                _resident((1, KV_LORA)), _resident(w_kv.shape)]
    args = [x, mod, perm, g_pre, w_in, g_q, w_q, g_kv, w_kv]
    if grp.latent:
        in_specs += [grp.by_position(Q_ROPE_ALL), grp.by_position(Q_ROPE_ALL), grp.by_position(LANES)]
        args += list(rope_tabs)
    return pl.pallas_call(
        functools.partial(_in_proj_kernel, grp.latent),
        grid=grp.grid,
        in_specs=in_specs,
        out_specs=[grp.seq_major(Q_COLS), grp.seq_major(KV_LORA), grp.seq_major(KV_COLS), grp.seq_major(QK_ROPE),
                   grp.time_major(LRU_WIDTH), grp.time_major(LRU_WIDTH), grp.seq_major(2 * D_MODEL)],
        out_shape=[grp.shape3(Q_COLS, BF16), grp.shape3(KV_LORA, F32), grp.shape3(KV_COLS, BF16),
                   grp.shape3(QK_ROPE, F32), grp.shape_tm(LRU_WIDTH, F32), grp.shape_tm(LRU_WIDTH, F32),
                   grp.shape3(2 * D_MODEL, F32)],
        compiler_params=_params(2),
        name="in_proj",
    )(*args)


def _kv_up_kernel(c_ref, w_ref, o_ref):
    o_ref[...] = _dot(c_ref[...].astype(BF16), w_ref[...]).astype(BF16)


def _kv_up(ckv, w_kv, rows_per_step):
    n = ckv.shape[0]
    return pl.pallas_call(
        _kv_up_kernel,
        grid=(n // rows_per_step,),
        in_specs=[pl.BlockSpec((rows_per_step, KV_LORA), lambda i: (i, 0)), _resident(w_kv.shape)],
        out_specs=pl.BlockSpec((rows_per_step, KV_COLS), lambda i: (i, 0)),
        out_shape=jax.ShapeDtypeStruct((n, KV_COLS), BF16),
        compiler_params=_params(1),
        name="kv_up_ctx",
    )(ckv, w_kv)


def _attn_kernel(with_ctx, q_ref, kv_ref, kr_ref, *rest):
    if with_ctx:
        kvc_ref, krc_ref, o_ref = rest
    else:
        (o_ref,) = rest
    scale = QK_DIM ** -0.5
    kr = kr_ref[...].astype(BF16)
    if with_ctx:
        krc = krc_ref[...].astype(BF16)
    for h in range(N_HEADS):
        nope = slice(h * QK_NOPE, (h + 1) * QK_NOPE)
        rope = slice(Q_NOPE_ALL + h * QK_ROPE, Q_NOPE_ALL + (h + 1) * QK_ROPE)
        val = slice(Q_NOPE_ALL + h * V_DIM, Q_NOPE_ALL + (h + 1) * V_DIM)
        qn, qr = q_ref[:, nope], q_ref[:, rope]
        s = (_dot_t(qn, kv_ref[:, nope]) + _dot_t(qr, kr)) * scale
        m = jnp.max(s, axis=-1, keepdims=True)
        if with_ctx:
            sc = (_dot_t(qn, kvc_ref[:, nope]) + _dot_t(qr, krc)) * scale
            m = jnp.maximum(m, jnp.max(sc, axis=-1, keepdims=True))
        p = jnp.exp(s - m)
        den = jnp.sum(p, axis=-1, keepdims=True)
        o = _dot(p.astype(BF16), kv_ref[:, val])
        if with_ctx:
            pc = jnp.exp(sc - m)
            den = den + jnp.sum(pc, axis=-1, keepdims=True)
            o = o + _dot(pc.astype(BF16), kvc_ref[:, val])
        o_ref[:, h * V_DIM:(h + 1) * V_DIM] = (o / den).astype(BF16)


def _attention(grp, q, kv, kr, kv_ctx=None, kr_ctx=None):
    with_ctx = kv_ctx is not None
    nq = grp.seq_len // Q_BLOCK
    seq = lambda rows, width: pl.BlockSpec((rows, width), lambda b, i: (b, 0))
    in_specs = [pl.BlockSpec((Q_BLOCK, Q_COLS), lambda b, i: (b * nq + i, 0)),
                seq(grp.seq_len, KV_COLS), seq(grp.seq_len, QK_ROPE)]
    args = [q, kv, kr]
    if with_ctx:
        past = kv_ctx.shape[0] // grp.batch
        in_specs += [seq(past, KV_COLS), seq(past, QK_ROPE)]
        args += [kv_ctx, kr_ctx]
    return pl.pallas_call(
        functools.partial(_attn_kernel, with_ctx),
        grid=(grp.batch, nq),
        in_specs=in_specs,
        out_specs=pl.BlockSpec((Q_BLOCK, ATTN_OUT), lambda b, i: (b * nq + i, 0)),
        out_shape=jax.ShapeDtypeStruct((grp.tokens, ATTN_OUT), BF16),
        compiler_params=_params(2),
        name="attention",
    )(*args)


def _softplus(z):
    return jnp.maximum(z, 0.0) + jnp.log1p(jnp.exp(-jnp.abs(z)))


def _lru_kernel(n_chunks, chunk_len, with_state, x_ref, gg_ref, cw_ref, cb_ref, wrg_ref, brg_ref,
                wig_ref, big_ref, lam_ref, hf0_ref, hb0_ref, *rest):
    if with_state:
        s_ref, hf_ref, hb_ref, xpad, a_f, u_f, a_b, u_b = rest
    else:
        s_ref, xpad, a_f, u_f, a_b, u_b = rest
    rows = chunk_len * SUBLANES
    halo = (CONV_W // 2) * SUBLANES
    sub = lax.broadcasted_iota(jnp.int32, (SUBLANES, LRU_CH), 0)
    first_chunk = (sub % n_chunks) == 0
    last_chunk = (sub % n_chunks) == n_chunks - 1

    def from_prev(v):
        return jnp.where(first_chunk, 0.0, pltpu.roll(v, 1, 0))

    def from_next(v):
        return jnp.where(last_chunk, 0.0, pltpu.roll(v, SUBLANES - 1, 0))

    xpad[halo:halo + rows, :] = x_ref[...]
    xpad[0:SUBLANES, :] = from_prev(x_ref[rows - 2 * SUBLANES:rows - SUBLANES, :])
    xpad[SUBLANES:halo, :] = from_prev(x_ref[rows - SUBLANES:rows, :])
    xpad[halo + rows:halo + rows + SUBLANES, :] = from_next(x_ref[0:SUBLANES, :])

    def gates(c, carry):
        r0 = pl.multiple_of(c * LRU_ROWS, LRU_ROWS)
        xc = cb_ref[...]
        for k in range(CONV_W):
            xc = xc + xpad[pl.ds(r0 + k * SUBLANES, LRU_ROWS), :] * cw_ref[k:k + 1, :]
        xcb = xc.astype(BF16)
        for d, (a_ref, u_ref) in enumerate(((a_f, u_f), (a_b, u_b))):
            for n in range(LRU_CH // LRU_BW):
                sl = slice(n * LRU_BW, (n + 1) * LRU_BW)
                r = jax.nn.sigmoid(_dot(xcb[:, sl], wrg_ref[d, n]) + brg_ref[d:d + 1, sl])
                i = jax.nn.sigmoid(_dot(xcb[:, sl], wig_ref[d, n]) + big_ref[d:d + 1, sl])
                log_a = (-LRU_C * _softplus(-lam_ref[d:d + 1, sl])) * r
                a = jnp.exp(log_a)
                a_ref[pl.ds(r0, LRU_ROWS), sl] = a
                one_minus_a2 = -jnp.tanh(log_a) * (a * a + 1.0)
                u_ref[pl.ds(r0, LRU_ROWS), sl] = jnp.sqrt(one_minus_a2) * i * xc[:, sl]
        return carry

    lax.fori_loop(0, rows // LRU_ROWS, gates, 0)

    def scan(hf, hb, store):
        def step(t, carry):
            hf, hb = carry
            rf = pl.multiple_of(t * SUBLANES, SUBLANES)
            rb = pl.multiple_of((chunk_len - 1 - t) * SUBLANES, SUBLANES)
            hf = a_f[pl.ds(rf, SUBLANES), :] * hf + u_f[pl.ds(rf, SUBLANES), :]
            hb = a_b[pl.ds(rb, SUBLANES), :] * hb + u_b[pl.ds(rb, SUBLANES), :]
            if store:
                u_f[pl.ds(rf, SUBLANES), :] = hf
                u_b[pl.ds(rb, SUBLANES), :] = hb
            return hf, hb
        return lax.fori_loop(0, chunk_len, step, (hf, hb), unroll=SCAN_UNROLL)

    hf, hb = hf0_ref[...], hb0_ref[...]
    if n_chunks > 1:
        ef, eb = scan(hf, hb, False)
        hf = jnp.where(first_chunk, hf, pltpu.roll(ef, 1, 0))
        hb = jnp.where(last_chunk, hb, pltpu.roll(eb, SUBLANES - 1, 0))
    hf, hb = scan(hf, hb, True)
    if with_state:
        hf_ref[...] = hf
        hb_ref[...] = hb

    def combine(c, carry):
        r = pl.ds(pl.multiple_of(c * LRU_ROWS, LRU_ROWS), LRU_ROWS)
        s_ref[r, :] = ((u_f[r, :] + u_b[r, :]) * gg_ref[r, :]).astype(BF16)
        return carry

    lax.fori_loop(0, rows // LRU_ROWS, combine, 0)


def _rg_lru(grp, lx, gg, conv_w, conv_b, w_rg, b_rg, w_ig, b_ig, lam, hf0, hb0, with_state):
    assert grp.n_chunks == 1 or not with_state
    rows = grp.chunk_len * SUBLANES
    nb = LRU_CH // LRU_BW
    tile = pl.BlockSpec((rows, LRU_CH), lambda g, c: (g, c))
    chan = lambda r: pl.BlockSpec((r, LRU_CH), lambda g, c: (0, c))
    wblk = pl.BlockSpec((2, nb, LRU_BW, LRU_BW), lambda g, c: (0, c, 0, 0))
    state = pl.BlockSpec((SUBLANES, LRU_CH), lambda g, c: (g, c))
    out_specs = [tile]
    out_shape = [grp.shape_tm(LRU_WIDTH, BF16)]
    if with_state:
        out_specs += [state, state]
        out_shape += [jax.ShapeDtypeStruct((grp.n_pseudo, LRU_WIDTH), F32)] * 2
    scratch = [pltpu.VMEM((rows + (CONV_W - 1) * SUBLANES, LRU_CH), F32)] + [pltpu.VMEM((rows, LRU_CH), F32)] * 4
    return pl.pallas_call(
        functools.partial(_lru_kernel, grp.n_chunks, grp.chunk_len, with_state),
        grid=(grp.lru_groups, LRU_WIDTH // LRU_CH),
        in_specs=[tile, tile, chan(CONV_W), chan(1), wblk, chan(2), wblk, chan(2), chan(2), state, state],
        out_specs=out_specs,
        out_shape=out_shape,
        scratch_shapes=scratch,
        compiler_params=_params(2),
        name="rg_lru",
    )(lx, gg, conv_w, conv_b, w_rg, b_rg, w_ig, b_ig, lam, hf0, hb0)


def _merge_kernel(o_ref, s_ref, gates_ref, x_ref, mod_ref, perm_ref, gpost_ref, wom_ref, wol_ref, wout_ref, y_ref):
    y_mla = _dot(_flat(o_ref[...]), wom_ref[...])
    s = _dot(perm_ref[...], s_ref[...]).astype(BF16)
    y_lru = _dot(s, wol_ref[...])
    gates = _flat(gates_ref[...])
    z = gates[:, :D_MODEL] * y_mla + gates[:, D_MODEL:] * y_lru
    y = _dot(z.astype(BF16), wout_ref[...])
    g1 = mod_ref[:, :, 2 * D_MODEL:3 * D_MODEL]
    y_ref[...] = x_ref[...] + g1 * _tiled(_rms(y, gpost_ref[...]))


def _merge(grp, o, s, gates, x, mod, perm_t, g_post, w_o_mla, w_o_lru, w_out):
    return pl.pallas_call(
        _merge_kernel,
        grid=grp.grid,
        in_specs=[grp.seq_major(ATTN_OUT), grp.time_major(LRU_WIDTH), grp.seq_major(2 * D_MODEL),
                  grp.seq_major(D_MODEL), grp.per_pseudo(N_MOD * D_MODEL), _resident(perm_t.shape),
                  _resident((1, D_MODEL)), _resident(w_o_mla.shape), _resident(w_o_lru.shape),
                  _resident(w_out.shape)],
        out_specs=grp.seq_major(D_MODEL),
        out_shape=grp.shape3(D_MODEL, F32),
        compiler_params=_params(2),
        name="merge",
    )(o, s, gates, x, mod, perm_t, g_post, w_o_mla, w_o_lru, w_out)


def _ffn_kernel(x_ref, mod_ref, gpre_ref, gpost_ref, w1_ref, w2_ref, y_ref):
    x = x_ref[...]
    sh2, sc2, g2 = (mod_ref[:, :, k * D_MODEL:(k + 1) * D_MODEL] for k in (3, 4, 5))
    hb = _flat((_rms(x, gpre_ref[...]) * (1.0 + sc2) + sh2).astype(BF16))
    acc = jnp.zeros((TILE_ROWS, D_MODEL), F32)
    for c in range(FF_HIDDEN // FFN_CHUNK):
        lo = c * FFN_CHUNK
        g = _dot(hb, w1_ref[:, lo:lo + FFN_CHUNK])
        u = _dot(hb, w1_ref[:, FF_HIDDEN + lo:FF_HIDDEN + lo + FFN_CHUNK])
        act = (g * jax.nn.sigmoid(g) * u).astype(BF16)
        acc = acc + _dot(act, w2_ref[lo:lo + FFN_CHUNK, :])
    y_ref[...] = x + g2 * _tiled(_rms(acc, gpost_ref[...]))


def _ffn(grp, x, mod, g_pre, g_post, w1, w2):
    return pl.pallas_call(
        _ffn_kernel,
        grid=grp.grid,
        in_specs=[grp.seq_major(D_MODEL), grp.per_pseudo(N_MOD * D_MODEL), _resident((1, D_MODEL)),
                  _resident((1, D_MODEL)), _resident(w1.shape), _resident(w2.shape)],
        out_specs=grp.seq_major(D_MODEL),
        out_shape=grp.shape3(D_MODEL, F32),
        compiler_params=_params(2),
        name="ffn",
    )(x, mod, g_pre, g_post, w1, w2)


def _rot_partner(w):
    half = QK_ROPE // 2
    return jnp.concatenate([-w[..., half:], w[..., :half]], axis=-1)


def _rope_tables(grp):
    n_tokens = grp.seq_len
    rows = n_tokens // GRID_W
    row = jnp.repeat(jnp.arange(rows, dtype=F32), GRID_W)
    col = jnp.tile(jnp.arange(GRID_W, dtype=F32), rows)
    n_freq = QK_ROPE // 4
    inv = ROPE_THETA ** (-jnp.arange(n_freq, dtype=F32) / n_freq)
    ang = jnp.concatenate([row[:, None] * inv, col[:, None] * inv], axis=-1)
    cos, sin = jnp.cos(ang), jnp.sin(ang)
    cos2, sin2 = jnp.concatenate([cos, cos], -1), jnp.concatenate([sin, sin], -1)
    ktab = jnp.concatenate([cos2, sin2, jnp.zeros((n_tokens, LANES - 2 * QK_ROPE), F32)], -1)

    def by_pseudo(tab):
        tab = tab.reshape(grp.n_chunks, grp.chunk_len, tab.shape[-1])
        return jnp.tile(tab, (SUBLANES // grp.n_chunks, 1, 1))

    return by_pseudo(jnp.tile(cos2, (1, N_HEADS))), by_pseudo(jnp.tile(sin2, (1, N_HEADS))), by_pseudo(ktab)


def kernel(x_prompt, x_sample, cache_ckv, cache_krope, state_lru_fwd, state_lru_bwd, c, c_ctx, w_mod, b_mod, g_pre_mix, g_post_mix, g_pre_ffn, g_post_ffn, w_in, g_q, w_q_up, g_kv, w_kv_up, w_o_mla, conv_w, conv_b, w_rg, b_rg, w_ig, b_ig, lru_lambda, w_o_lru, w_out, w_ffn_in, w_ffn_out):
    batch, seq = x_prompt.shape[:2]
    dec_batch, dec_seq = x_sample.shape[:2]
    past = cache_ckv.shape[2]

    off_kr = Q_LORA + KV_LORA
    w_kr = w_in[:, :, off_kr:off_kr + QK_ROPE]
    w_in_r = jnp.concatenate(
        [w_in[:, :, :off_kr], w_in[:, :, off_kr + QK_ROPE:], w_kr, _rot_partner(w_kr),
         jnp.zeros((DEPTH, D_MODEL, LANES - 2 * QK_ROPE), F32)], axis=-1).astype(BF16)
    wq = w_q_up.reshape(DEPTH, Q_LORA, N_HEADS, QK_DIM)
    wq_rope = wq[..., QK_NOPE:]
    w_q_r = jnp.concatenate([wq[..., :QK_NOPE].reshape(DEPTH, Q_LORA, Q_NOPE_ALL),
                             wq_rope.reshape(DEPTH, Q_LORA, Q_ROPE_ALL),
                             _rot_partner(wq_rope).reshape(DEPTH, Q_LORA, Q_ROPE_ALL)], axis=-1).astype(BF16)
    wkv = w_kv_up.reshape(DEPTH, KV_LORA, N_HEADS, QK_NOPE + V_DIM)
    w_kv_r = jnp.concatenate([wkv[..., :QK_NOPE].reshape(DEPTH, KV_LORA, Q_NOPE_ALL),
                              wkv[..., QK_NOPE:].reshape(DEPTH, KV_LORA, ATTN_OUT)], axis=-1).astype(BF16)
    w_o_mla_b, w_o_lru_b, w_out_b = w_o_mla.astype(BF16), w_o_lru.astype(BF16), w_out.astype(BF16)
    w_rg_b, w_ig_b = w_rg.astype(BF16), w_ig.astype(BF16)
    w1_b, w2_b = w_ffn_in.astype(BF16), w_ffn_out.astype(BF16)
    row = lambda v: v.reshape(DEPTH, 1, -1)
    g_pre_mix, g_post_mix, g_pre_ffn, g_post_ffn = map(row, (g_pre_mix, g_post_mix, g_pre_ffn, g_post_ffn))
    g_q, g_kv, conv_b = row(g_q), row(g_kv), row(conv_b)
    perm = _to_time_major()
    perm_b, perm_t_b = jnp.asarray(perm, BF16), jnp.asarray(perm.T, BF16)

    mod_all = _modulation(jnp.concatenate([c_ctx[None, :], c], axis=0), w_mod, b_mod)

    def run_group(grp, x, mod_rows, layer_inputs):
        x = x.reshape(grp.n_pseudo, grp.chunk_len, D_MODEL)
        rope_tabs = _rope_tables(grp) if grp.latent else None
        flat = lambda a: a.reshape(grp.tokens, a.shape[-1])
        per_layer = []
        for l in range(DEPTH):
            mod = mod_all[l][mod_rows][:, None, :]
            w_q_l = w_q_r[l] if grp.latent else w_q_r[l, :, :Q_COLS]
            q, ckv, kv, kr, lx, gg, gates = _in_proj(grp, x, mod, perm_b, g_pre_mix[l], w_in_r[l], g_q[l], w_q_l,
                                                     g_kv[l], w_kv_r[l], rope_tabs)
            hf0, hb0, kv_ctx, kr_ctx = layer_inputs(l, w_kv_r[l])
            o = _attention(grp, flat(q), flat(kv), flat(kr), kv_ctx, kr_ctx)
            lru = _rg_lru(grp, lx, gg, conv_w[l], conv_b[l], w_rg_b[l], b_rg[l], w_ig_b[l], b_ig[l], lru_lambda[l],
                          hf0, hb0, not grp.latent)
            o = o.reshape(grp.n_pseudo, grp.chunk_len, ATTN_OUT)
            x = _merge(grp, o, lru[0], gates, x, mod, perm_t_b, g_post_mix[l], w_o_mla_b[l], w_o_lru_b[l], w_out_b[l])
            x = _ffn(grp, x, mod, g_pre_ffn[l], g_post_ffn[l], w1_b[l], w2_b[l])
            per_layer.append((ckv, kr) + tuple(lru[1:]))
        return x, per_layer

    ctx = _Group(batch, seq, latent=False)
    zeros_state = jnp.zeros((batch, LRU_WIDTH), F32)
    y_prompt, ctx_layers = run_group(ctx, x_prompt, np.zeros(ctx.n_pseudo, np.int32),
                                     lambda l, w: (zeros_state, zeros_state, None, None))
    stack = lambda k, shape: jnp.stack([lay[k].reshape(shape) for lay in ctx_layers], axis=1)
    new_ckv = stack(0, (batch, seq, KV_LORA))
    new_krope = stack(1, (batch, seq, QK_ROPE))
    new_lru_fwd = stack(2, (batch, LRU_WIDTH))
    new_lru_bwd = stack(3, (batch, LRU_WIDTH))

    lat = _Group(dec_batch, dec_seq, latent=True)

    def latent_inputs(l, w_kv_l):
        per_chunk = lambda s: jnp.repeat(s[:, l], lat.n_chunks, axis=0)
        kv_ctx = _kv_up(cache_ckv[:, l].reshape(dec_batch * past, KV_LORA), w_kv_l, past)
        return (per_chunk(state_lru_fwd), per_chunk(state_lru_bwd), kv_ctx,
                cache_krope[:, l].reshape(dec_batch * past, QK_ROPE))

    y_sample, _ = run_group(lat, x_sample, 1 + np.arange(lat.n_pseudo) // lat.n_chunks, latent_inputs)
    return (y_prompt.reshape(batch, seq, D_MODEL), y_sample.reshape(dec_batch, dec_seq, D_MODEL),
            new_ckv, new_krope, new_lru_fwd, new_lru_bwd)
```

```python
import functools

import numpy as np

import jax
import jax.numpy as jnp
from jax import lax
from jax.experimental import pallas as pl
from jax.experimental.pallas import tpu as pltpu

D_MODEL = 1024
DEPTH = 2
GRID_W = 64
N_HEADS = 8
QK_NOPE = 64
QK_ROPE = 32
QK_DIM = QK_NOPE + QK_ROPE
V_DIM = 64
Q_LORA = 384
KV_LORA = 256
ROPE_THETA = 10000.0
LRU_WIDTH = 1024
LRU_BLOCKS = 8
LRU_BW = LRU_WIDTH // LRU_BLOCKS
CONV_W = 4
LRU_C = 8.0
FF_HIDDEN = 2816
N_MOD = 6
EPS = 1e-6

SUBLANES = 8
LANES = 128
VMEM_LIMIT_BYTES = 48 * 1024 * 1024

C_Q = 0
C_KV = C_Q + Q_LORA
C_LX = C_KV + KV_LORA
C_LG = C_LX + LRU_WIDTH
C_MG = C_LG + LRU_WIDTH
C_KR = C_MG + 2 * D_MODEL
IN_COLS_PADDED = C_KR + LANES
HEAD_LANES = LANES
assert QK_DIM <= HEAD_LANES and QK_NOPE + V_DIM == HEAD_LANES
Q_COLS = N_HEADS * HEAD_LANES
KV_COLS = N_HEADS * HEAD_LANES
ATTN_OUT = N_HEADS * HEAD_LANES

STEPS = 32
TILE_ROWS = SUBLANES * STEPS
Q_BLOCK = 256
FFN_STEPS = 64
FFN_CHUNK = 256
LRU_CH = 256
LRU_ROWS = 512
SCAN_UNROLL = 8
MOD_TILE = 1536

F32 = jnp.float32
BF16 = jnp.bfloat16


def _params(n_grid_dims):
    return pltpu.CompilerParams(dimension_semantics=("arbitrary",) * n_grid_dims,
                                vmem_limit_bytes=VMEM_LIMIT_BYTES)


def _resident(shape):
    zeros = (0,) * len(shape)
    return pl.BlockSpec(shape, lambda *_: zeros, pipeline_mode=pl.Buffered(1))


def _rms(x, g):
    return x * lax.rsqrt(jnp.mean(x * x, axis=-1, keepdims=True) + EPS) * g


def _dot(a, b):
    return jnp.dot(a, b, preferred_element_type=F32)


def _dot_t(a, b):
    return lax.dot_general(a, b, (((1,), (1,)), ((), ())), preferred_element_type=F32)


def _flat(x3):
    return x3.reshape(x3.shape[0] * x3.shape[1], x3.shape[-1])


def _tiled(x2):
    return x2.reshape(SUBLANES, x2.shape[0] // SUBLANES, x2.shape[-1])


def _sigmoid(x):
    return 0.5 * jnp.tanh(0.5 * x) + 0.5


def _to_time_major():
    r = np.arange(TILE_ROWS)
    perm = np.zeros((TILE_ROWS, TILE_ROWS), np.float32)
    perm[r, (r % SUBLANES) * STEPS + r // SUBLANES] = 1.0
    return perm


class _Group:
    def __init__(self, batch, seq_len, latent):
        self.batch, self.seq_len, self.latent = batch, seq_len, latent
        self.tokens = batch * seq_len
        self.n_chunks = max(1, SUBLANES // batch)
        self.n_pseudo = batch * self.n_chunks
        assert self.n_chunks in (1, 2) and self.n_pseudo % SUBLANES == 0
        self.chunk_len = seq_len // self.n_chunks
        assert self.chunk_len % STEPS == 0 and seq_len % Q_BLOCK == 0
        self.lru_groups = self.n_pseudo // SUBLANES
        self.tiles_per_chunk = self.chunk_len // STEPS
        self.grid = (self.lru_groups, self.tiles_per_chunk)

    def seq_major(self, width, steps=STEPS):
        return pl.BlockSpec((SUBLANES, steps, width), lambda g, i: (g, i, 0))

    def time_major(self, width):
        return pl.BlockSpec((TILE_ROWS, width), lambda g, i: (g * self.tiles_per_chunk + i, 0))

    def per_pseudo(self, width):
        return pl.BlockSpec((SUBLANES, 1, width), lambda g, i: (g, 0, 0))

    def by_position(self, width):
        return pl.BlockSpec((SUBLANES, STEPS, width), lambda g, i: (0, i, 0))

    def shape3(self, width, dtype):
        return jax.ShapeDtypeStruct((self.n_pseudo, self.chunk_len, width), dtype)

    def shape_tm(self, width, dtype):
        return jax.ShapeDtypeStruct((self.lru_groups * self.chunk_len * SUBLANES, width), dtype)


def _mod_kernel(c_ref, w_ref, b_ref, o_ref):
    c = c_ref[...]
    a = (c * _sigmoid(c)).astype(BF16)
    o_ref[...] = _dot(a, w_ref[...].astype(BF16)) + b_ref[...]


def _modulation(cond, w_mod, b_mod):
    n = N_MOD * D_MODEL
    rows = cond.shape[0]
    return pl.pallas_call(
        _mod_kernel,
        grid=(DEPTH, n // MOD_TILE),
        in_specs=[pl.BlockSpec((rows, D_MODEL), lambda l, j: (0, 0)),
                  pl.BlockSpec((None, D_MODEL, MOD_TILE), lambda l, j: (l, 0, j)),
                  pl.BlockSpec((None, 1, MOD_TILE), lambda l, j: (l, 0, j))],
        out_specs=pl.BlockSpec((None, rows, MOD_TILE), lambda l, j: (l, 0, j)),
        out_shape=jax.ShapeDtypeStruct((DEPTH, rows, n), F32),
        compiler_params=_params(2),
        name="modulation",
    )(cond, w_mod, b_mod.reshape(DEPTH, 1, n))


def _in_proj_kernel(latent, x_ref, mod_ref, perm_ref, gpre_ref, win_ref, gq_ref, wq_ref, gkv_ref, wkv_ref, *rest):
    if latent:
        cosq_ref, sinq_ref, ktab_ref = rest[:3]
        rest = rest[3:]
    q_ref, ckv_ref, kv_ref, kr_ref, lx_ref, gg_ref, gates_ref = rest
    sh1, sc1 = mod_ref[:, :, 0:D_MODEL], mod_ref[:, :, D_MODEL:2 * D_MODEL]
    hb = _flat((_rms(x_ref[...], gpre_ref[...]) * (1.0 + sc1) + sh1).astype(BF16))

    def proj(lhs, lo, hi):
        return _dot(lhs, win_ref[:, lo:hi])

    qn = _rms(proj(hb, C_Q, C_KV), gq_ref[...]).astype(BF16)
    qf = _dot(qn, wq_ref[...])
    if latent:
        cos, sin = _flat(cosq_ref[...]), _flat(sinq_ref[...])
        for h in range(N_HEADS):
            lanes = slice(h * HEAD_LANES, (h + 1) * HEAD_LANES)
            partner = slice(Q_COLS + h * HEAD_LANES, Q_COLS + (h + 1) * HEAD_LANES)
            q_ref[:, :, lanes] = _tiled((qf[:, lanes] * cos + qf[:, partner] * sin).astype(BF16))
    else:
        q_ref[...] = _tiled((qf * QK_DIM ** -0.5).astype(BF16))

    ckv = _rms(proj(hb, C_KV, C_LX), gkv_ref[...])
    ckv_ref[...] = _tiled(ckv)
    kv_ref[...] = _tiled(_dot(ckv.astype(BF16), wkv_ref[...]).astype(BF16))

    kr = proj(hb, C_KR, IN_COLS_PADDED)
    if latent:
        pr = kr * _flat(ktab_ref[...])
        kr = pr + pltpu.roll(pr, LANES - QK_ROPE, 1)
    kr_ref[...] = _tiled(kr[:, :QK_ROPE])

    gates_ref[...] = _tiled(_sigmoid(proj(hb, C_MG, C_KR)))

    hb_tm = _dot(perm_ref[...], hb).astype(BF16)
    lx_ref[...] = proj(hb_tm, C_LX, C_LG)
    gg_ref[...] = jax.nn.gelu(proj(hb_tm, C_LG, C_MG))


def _in_proj(grp, x, mod, perm, g_pre, w_in, g_q, w_q, g_kv, w_kv, rope_tabs):
    in_specs = [grp.seq_major(D_MODEL), grp.per_pseudo(N_MOD * D_MODEL), _resident(perm.shape),
                _resident((1, D_MODEL)), _resident(w_in.shape), _resident((1, Q_LORA)), _resident(w_q.shape),
                _resident((1, KV_LORA)), _resident(w_kv.shape)]
    args = [x, mod, perm, g_pre, w_in, g_q, w_q, g_kv, w_kv]
    if grp.latent:
        in_specs += [grp.by_position(HEAD_LANES), grp.by_position(HEAD_LANES), grp.by_position(LANES)]
        args += list(rope_tabs)
    return pl.pallas_call(
        functools.partial(_in_proj_kernel, grp.latent),
        grid=grp.grid,
        in_specs=in_specs,
        out_specs=[grp.seq_major(Q_COLS), grp.seq_major(KV_LORA), grp.seq_major(KV_COLS), grp.seq_major(QK_ROPE),
                   grp.time_major(LRU_WIDTH), grp.time_major(LRU_WIDTH), grp.seq_major(2 * D_MODEL)],
        out_shape=[grp.shape3(Q_COLS, BF16), grp.shape3(KV_LORA, F32), grp.shape3(KV_COLS, BF16),
                   grp.shape3(QK_ROPE, F32), grp.shape_tm(LRU_WIDTH, F32), grp.shape_tm(LRU_WIDTH, F32),
                   grp.shape3(2 * D_MODEL, F32)],
        compiler_params=_params(2),
        name="in_proj",
    )(*args)


def _kv_up_kernel(c_ref, w_ref, o_ref):
    o_ref[...] = _dot(c_ref[...].astype(BF16), w_ref[...]).astype(BF16)


def _kv_up(ckv, w_kv, rows_per_step):
    n = ckv.shape[0]
    return pl.pallas_call(
        _kv_up_kernel,
        grid=(n // rows_per_step,),
        in_specs=[pl.BlockSpec((rows_per_step, KV_LORA), lambda i: (i, 0)), _resident(w_kv.shape)],
        out_specs=pl.BlockSpec((rows_per_step, KV_COLS), lambda i: (i, 0)),
        out_shape=jax.ShapeDtypeStruct((n, KV_COLS), BF16),
        compiler_params=_params(1),
        name="kv_up_ctx",
    )(ckv, w_kv)


def _attn_kernel(with_ctx, q_ref, kv_ref, kr_ref, place_ref, *rest):
    if with_ctx:
        kvc_ref, krc_ref, o_ref = rest
    else:
        (o_ref,) = rest
    is_nope = lax.broadcasted_iota(jnp.int32, (1, HEAD_LANES), 1) < QK_NOPE

    def keys_of(kv, kr):
        kr_placed = _dot(kr[...].astype(BF16), place_ref[...]).astype(BF16)
        return lambda h: jnp.where(is_nope, kv[:, h * HEAD_LANES:(h + 1) * HEAD_LANES], kr_placed)

    keys = keys_of(kv_ref, kr_ref)
    if with_ctx:
        keys_ctx = keys_of(kvc_ref, krc_ref)
    for h in range(N_HEADS):
        lanes = slice(h * HEAD_LANES, (h + 1) * HEAD_LANES)
        q = q_ref[:, lanes]
        s = _dot_t(q, keys(h))
        m = jnp.max(s, axis=-1, keepdims=True)
        if with_ctx:
            sc = _dot_t(q, keys_ctx(h))
            m = jnp.maximum(m, jnp.max(sc, axis=-1, keepdims=True))
        p = jnp.exp(s - m)
        den = jnp.sum(p, axis=-1, keepdims=True)
        o = _dot(p.astype(BF16), kv_ref[:, lanes])
        if with_ctx:
            pc = jnp.exp(sc - m)
            den = den + jnp.sum(pc, axis=-1, keepdims=True)
            o = o + _dot(pc.astype(BF16), kvc_ref[:, lanes])
        o_ref[:, lanes] = (o / den).astype(BF16)


def _attention(grp, q, kv, kr, kv_ctx=None, kr_ctx=None):
    with_ctx = kv_ctx is not None
    nq = grp.seq_len // Q_BLOCK
    seq = lambda rows, width: pl.BlockSpec((rows, width), lambda b, i: (b, 0))
    place = np.zeros((QK_ROPE, HEAD_LANES), np.float32)
    place[np.arange(QK_ROPE), QK_NOPE + np.arange(QK_ROPE)] = 1.0
    in_specs = [pl.BlockSpec((Q_BLOCK, Q_COLS), lambda b, i: (b * nq + i, 0)),
                seq(grp.seq_len, KV_COLS), seq(grp.seq_len, QK_ROPE), _resident(place.shape)]
    args = [q, kv, kr, jnp.asarray(place, BF16)]
    if with_ctx:
        past = kv_ctx.shape[0] // grp.batch
        in_specs += [seq(past, KV_COLS), seq(past, QK_ROPE)]
        args += [kv_ctx, kr_ctx]
    return pl.pallas_call(
        functools.partial(_attn_kernel, with_ctx),
        grid=(grp.batch, nq),
        in_specs=in_specs,
        out_specs=pl.BlockSpec((Q_BLOCK, ATTN_OUT), lambda b, i: (b * nq + i, 0)),
        out_shape=jax.ShapeDtypeStruct((grp.tokens, ATTN_OUT), BF16),
        compiler_params=_params(2),
        name="attention",
    )(*args)


def _softplus(z):
    return jnp.maximum(z, 0.0) + jnp.log1p(jnp.exp(-jnp.abs(z)))


def _lru_kernel(n_chunks, chunk_len, with_state, x_ref, gg_ref, cw_ref, cb_ref, wrg_ref, brg_ref,
                wig_ref, big_ref, lam_ref, hf0_ref, hb0_ref, *rest):
    if with_state:
        s_ref, hf_ref, hb_ref, xpad, a_f, u_f, a_b, u_b = rest
    else:
        s_ref, xpad, a_f, u_f, a_b, u_b = rest
    rows = chunk_len * SUBLANES
    halo = (CONV_W // 2) * SUBLANES
    sub = lax.broadcasted_iota(jnp.int32, (SUBLANES, LRU_CH), 0)
    first_chunk = (sub % n_chunks) == 0
    last_chunk = (sub % n_chunks) == n_chunks - 1

    def from_prev(v):
        return jnp.where(first_chunk, 0.0, pltpu.roll(v, 1, 0))

    def from_next(v):
        return jnp.where(last_chunk, 0.0, pltpu.roll(v, SUBLANES - 1, 0))

    xpad[halo:halo + rows, :] = x_ref[...]
    xpad[0:SUBLANES, :] = from_prev(x_ref[rows - 2 * SUBLANES:rows - SUBLANES, :])
    xpad[SUBLANES:halo, :] = from_prev(x_ref[rows - SUBLANES:rows, :])
    xpad[halo + rows:halo + rows + SUBLANES, :] = from_next(x_ref[0:SUBLANES, :])

    def gates(c, carry):
        r0 = pl.multiple_of(c * LRU_ROWS, LRU_ROWS)
        xc = cb_ref[...]
        for k in range(CONV_W):
            xc = xc + xpad[pl.ds(r0 + k * SUBLANES, LRU_ROWS), :] * cw_ref[k:k + 1, :]
        xcb = xc.astype(BF16)
        for d, (a_ref, u_ref) in enumerate(((a_f, u_f), (a_b, u_b))):
            for n in range(LRU_CH // LRU_BW):
                sl = slice(n * LRU_BW, (n + 1) * LRU_BW)
                r = _sigmoid(_dot(xcb[:, sl], wrg_ref[d, n]) + brg_ref[d:d + 1, sl])
                i = _sigmoid(_dot(xcb[:, sl], wig_ref[d, n]) + big_ref[d:d + 1, sl])
                log_a = (-LRU_C * _softplus(-lam_ref[d:d + 1, sl])) * r
                a = jnp.exp(log_a)
                a_ref[pl.ds(r0, LRU_ROWS), sl] = a
                one_minus_a2 = -jnp.tanh(log_a) * (a * a + 1.0)
                u_ref[pl.ds(r0, LRU_ROWS), sl] = jnp.sqrt(one_minus_a2) * i * xc[:, sl]
        return carry

    lax.fori_loop(0, rows // LRU_ROWS, gates, 0)

    def scan(hf, hb, store):
        def step(t, carry):
            hf, hb = carry
            rf = pl.multiple_of(t * SUBLANES, SUBLANES)
            rb = pl.multiple_of((chunk_len - 1 - t) * SUBLANES, SUBLANES)
            hf = a_f[pl.ds(rf, SUBLANES), :] * hf + u_f[pl.ds(rf, SUBLANES), :]
            hb = a_b[pl.ds(rb, SUBLANES), :] * hb + u_b[pl.ds(rb, SUBLANES), :]
            if store:
                u_f[pl.ds(rf, SUBLANES), :] = hf
                u_b[pl.ds(rb, SUBLANES), :] = hb
            return hf, hb
        return lax.fori_loop(0, chunk_len, step, (hf, hb), unroll=SCAN_UNROLL)

    hf, hb = hf0_ref[...], hb0_ref[...]
    if n_chunks > 1:
        ef, eb = scan(hf, hb, False)
        hf = jnp.where(first_chunk, hf, pltpu.roll(ef, 1, 0))
        hb = jnp.where(last_chunk, hb, pltpu.roll(eb, SUBLANES - 1, 0))
    hf, hb = scan(hf, hb, True)
    if with_state:
        hf_ref[...] = hf
        hb_ref[...] = hb

    def combine(c, carry):
        r = pl.ds(pl.multiple_of(c * LRU_ROWS, LRU_ROWS), LRU_ROWS)
        s_ref[r, :] = ((u_f[r, :] + u_b[r, :]) * gg_ref[r, :]).astype(BF16)
        return carry

    lax.fori_loop(0, rows // LRU_ROWS, combine, 0)


def _rg_lru(grp, lx, gg, conv_w, conv_b, w_rg, b_rg, w_ig, b_ig, lam, hf0, hb0, with_state):
    assert grp.n_chunks == 1 or not with_state
    rows = grp.chunk_len * SUBLANES
    nb = LRU_CH // LRU_BW
    tile = pl.BlockSpec((rows, LRU_CH), lambda g, c: (g, c))
    chan = lambda r: pl.BlockSpec((r, LRU_CH), lambda g, c: (0, c))
    wblk = pl.BlockSpec((2, nb, LRU_BW, LRU_BW), lambda g, c: (0, c, 0, 0))
    state = pl.BlockSpec((SUBLANES, LRU_CH), lambda g, c: (g, c))
    out_specs = [tile]
    out_shape = [grp.shape_tm(LRU_WIDTH, BF16)]
    if with_state:
        out_specs += [state, state]
        out_shape += [jax.ShapeDtypeStruct((grp.n_pseudo, LRU_WIDTH), F32)] * 2
    scratch = [pltpu.VMEM((rows + (CONV_W - 1) * SUBLANES, LRU_CH), F32)] + [pltpu.VMEM((rows, LRU_CH), F32)] * 4
    return pl.pallas_call(
        functools.partial(_lru_kernel, grp.n_chunks, grp.chunk_len, with_state),
        grid=(grp.lru_groups, LRU_WIDTH // LRU_CH),
        in_specs=[tile, tile, chan(CONV_W), chan(1), wblk, chan(2), wblk, chan(2), chan(2), state, state],
        out_specs=out_specs,
        out_shape=out_shape,
        scratch_shapes=scratch,
        compiler_params=_params(2),
        name="rg_lru",
    )(lx, gg, conv_w, conv_b, w_rg, b_rg, w_ig, b_ig, lam, hf0, hb0)


def _merge_kernel(o_ref, s_ref, gates_ref, x_ref, mod_ref, perm_ref, gpost_ref, wom_ref, wol_ref, wout_ref, y_ref):
    y_mla = _dot(_flat(o_ref[...]), wom_ref[...])
    s = _dot(perm_ref[...], s_ref[...]).astype(BF16)
    y_lru = _dot(s, wol_ref[...])
    gates = _flat(gates_ref[...])
    z = gates[:, :D_MODEL] * y_mla + gates[:, D_MODEL:] * y_lru
    y = _dot(z.astype(BF16), wout_ref[...])
    g1 = mod_ref[:, :, 2 * D_MODEL:3 * D_MODEL]
    y_ref[...] = x_ref[...] + g1 * _tiled(_rms(y, gpost_ref[...]))


def _merge(grp, o, s, gates, x, mod, perm_t, g_post, w_o_mla, w_o_lru, w_out):
    return pl.pallas_call(
        _merge_kernel,
        grid=grp.grid,
        in_specs=[grp.seq_major(ATTN_OUT), grp.time_major(LRU_WIDTH), grp.seq_major(2 * D_MODEL),
                  grp.seq_major(D_MODEL), grp.per_pseudo(N_MOD * D_MODEL), _resident(perm_t.shape),
                  _resident((1, D_MODEL)), _resident(w_o_mla.shape), _resident(w_o_lru.shape),
                  _resident(w_out.shape)],
        out_specs=grp.seq_major(D_MODEL),
        out_shape=grp.shape3(D_MODEL, F32),
        compiler_params=_params(2),
        name="merge",
    )(o, s, gates, x, mod, perm_t, g_post, w_o_mla, w_o_lru, w_out)


def _ffn_kernel(x_ref, mod_ref, gpre_ref, gpost_ref, w1_ref, w2_ref, y_ref, act_ref):
    x = x_ref[...]
    sh2, sc2, g2 = (mod_ref[:, :, k * D_MODEL:(k + 1) * D_MODEL] for k in (3, 4, 5))
    hb = _flat((_rms(x, gpre_ref[...]) * (1.0 + sc2) + sh2).astype(BF16))
    for c in range(FF_HIDDEN // FFN_CHUNK):
        lo = c * FFN_CHUNK
        g = _dot(hb, w1_ref[:, lo:lo + FFN_CHUNK])
        u = _dot(hb, w1_ref[:, FF_HIDDEN + lo:FF_HIDDEN + lo + FFN_CHUNK])
        act_ref[:, lo:lo + FFN_CHUNK] = (g * _sigmoid(g) * u).astype(BF16)
    y = _dot(act_ref[...], w2_ref[...])
    y_ref[...] = x + g2 * _tiled(_rms(y, gpost_ref[...]))


def _ffn(grp, x, mod, g_pre, g_post, w1, w2):
    assert grp.chunk_len % FFN_STEPS == 0
    return pl.pallas_call(
        _ffn_kernel,
        grid=(grp.lru_groups, grp.chunk_len // FFN_STEPS),
        in_specs=[grp.seq_major(D_MODEL, FFN_STEPS), grp.per_pseudo(N_MOD * D_MODEL), _resident((1, D_MODEL)),
                  _resident((1, D_MODEL)), _resident(w1.shape), _resident(w2.shape)],
        out_specs=grp.seq_major(D_MODEL, FFN_STEPS),
        out_shape=grp.shape3(D_MODEL, F32),
        scratch_shapes=[pltpu.VMEM((SUBLANES * FFN_STEPS, FF_HIDDEN), BF16)],
        compiler_params=_params(2),
        name="ffn",
    )(x, mod, g_pre, g_post, w1, w2)


def _rot_partner(w):
    half = QK_ROPE // 2
    return jnp.concatenate([-w[..., half:], w[..., :half]], axis=-1)


def _rope_tables(grp):
    n_tokens = grp.seq_len
    rows = n_tokens // GRID_W
    row = jnp.repeat(jnp.arange(rows, dtype=F32), GRID_W)
    col = jnp.tile(jnp.arange(GRID_W, dtype=F32), rows)
    n_freq = QK_ROPE // 4
    inv = ROPE_THETA ** (-jnp.arange(n_freq, dtype=F32) / n_freq)
    ang = jnp.concatenate([row[:, None] * inv, col[:, None] * inv], axis=-1)
    cos, sin = jnp.cos(ang), jnp.sin(ang)
    cos2, sin2 = jnp.concatenate([cos, cos], -1), jnp.concatenate([sin, sin], -1)
    ktab = jnp.concatenate([cos2, sin2, jnp.zeros((n_tokens, LANES - 2 * QK_ROPE), F32)], -1)
    scale = QK_DIM ** -0.5
    tail = jnp.zeros((n_tokens, HEAD_LANES - QK_DIM), F32)
    qcos = jnp.concatenate([jnp.ones((n_tokens, QK_NOPE), F32), cos2, tail], -1) * scale
    qsin = jnp.concatenate([jnp.zeros((n_tokens, QK_NOPE), F32), sin2, tail], -1) * scale

    def by_pseudo(tab):
        tab = tab.reshape(grp.n_chunks, grp.chunk_len, tab.shape[-1])
        return jnp.tile(tab, (SUBLANES // grp.n_chunks, 1, 1))

    return by_pseudo(qcos), by_pseudo(qsin), by_pseudo(ktab)


def kernel(x_prompt, x_sample, cache_ckv, cache_krope, state_lru_fwd, state_lru_bwd, c, c_ctx, w_mod, b_mod, g_pre_mix, g_post_mix, g_pre_ffn, g_post_ffn, w_in, g_q, w_q_up, g_kv, w_kv_up, w_o_mla, conv_w, conv_b, w_rg, b_rg, w_ig, b_ig, lru_lambda, w_o_lru, w_out, w_ffn_in, w_ffn_out):
    batch, seq = x_prompt.shape[:2]
    dec_batch, dec_seq = x_sample.shape[:2]
    past = cache_ckv.shape[2]

    off_kr = Q_LORA + KV_LORA
    w_kr = w_in[:, :, off_kr:off_kr + QK_ROPE]
    w_in_r = jnp.concatenate(
        [w_in[:, :, :off_kr], w_in[:, :, off_kr + QK_ROPE:], w_kr, _rot_partner(w_kr),
         jnp.zeros((DEPTH, D_MODEL, LANES - 2 * QK_ROPE), F32)], axis=-1).astype(BF16)
    wq = w_q_up.reshape(DEPTH, Q_LORA, N_HEADS, QK_DIM)
    q_tail = jnp.zeros((DEPTH, Q_LORA, N_HEADS, HEAD_LANES - QK_DIM), F32)
    q_main = jnp.concatenate([wq, q_tail], axis=-1)
    q_partner = jnp.concatenate([jnp.zeros_like(wq[..., :QK_NOPE]), _rot_partner(wq[..., QK_NOPE:]), q_tail], axis=-1)
    w_q_r = jnp.concatenate([q_main.reshape(DEPTH, Q_LORA, Q_COLS),
                             q_partner.reshape(DEPTH, Q_LORA, Q_COLS)], axis=-1).astype(BF16)
    w_kv_r = w_kv_up.astype(BF16)
    wo = w_o_mla.reshape(DEPTH, N_HEADS, V_DIM, D_MODEL)
    w_o_mla_b = jnp.concatenate([jnp.zeros((DEPTH, N_HEADS, HEAD_LANES - V_DIM, D_MODEL), F32), wo],
                                axis=2).reshape(DEPTH, ATTN_OUT, D_MODEL).astype(BF16)
    w_o_lru_b, w_out_b = w_o_lru.astype(BF16), w_out.astype(BF16)
    w_rg_b, w_ig_b = w_rg.astype(BF16), w_ig.astype(BF16)
    w1_b, w2_b = w_ffn_in.astype(BF16), w_ffn_out.astype(BF16)
    row = lambda v: v.reshape(DEPTH, 1, -1)
    g_pre_mix, g_post_mix, g_pre_ffn, g_post_ffn = map(row, (g_pre_mix, g_post_mix, g_pre_ffn, g_post_ffn))
    g_q, g_kv, conv_b = row(g_q), row(g_kv), row(conv_b)
    perm = _to_time_major()
    perm_b, perm_t_b = jnp.asarray(perm, BF16), jnp.asarray(perm.T, BF16)

    mod_all = _modulation(jnp.concatenate([c_ctx[None, :], c], axis=0), w_mod, b_mod)

    def run_group(grp, x, mod_rows, layer_inputs):
        x = x.reshape(grp.n_pseudo, grp.chunk_len, D_MODEL)
        rope_tabs = _rope_tables(grp) if grp.latent else None
        flat = lambda a: a.reshape(grp.tokens, a.shape[-1])
        per_layer = []
        for l in range(DEPTH):
            mod = mod_all[l][mod_rows][:, None, :]
            w_q_l = w_q_r[l] if grp.latent else w_q_r[l, :, :Q_COLS]
            q, ckv, kv, kr, lx, gg, gates = _in_proj(grp, x, mod, perm_b, g_pre_mix[l], w_in_r[l], g_q[l], w_q_l,
                                                     g_kv[l], w_kv_r[l], rope_tabs)
            hf0, hb0, kv_ctx, kr_ctx = layer_inputs(l, w_kv_r[l])
            o = _attention(grp, flat(q), flat(kv), flat(kr), kv_ctx, kr_ctx)
            lru = _rg_lru(grp, lx, gg, conv_w[l], conv_b[l], w_rg_b[l], b_rg[l], w_ig_b[l], b_ig[l], lru_lambda[l],
                          hf0, hb0, not grp.latent)
            o = o.reshape(grp.n_pseudo, grp.chunk_len, ATTN_OUT)
            x = _merge(grp, o, lru[0], gates, x, mod, perm_t_b, g_post_mix[l], w_o_mla_b[l], w_o_lru_b[l], w_out_b[l])
            x = _ffn(grp, x, mod, g_pre_ffn[l], g_post_ffn[l], w1_b[l], w2_b[l])
            per_layer.append((ckv, kr) + tuple(lru[1:]))
        return x, per_layer

    ctx = _Group(batch, seq, latent=False)
    zeros_state = jnp.zeros((batch, LRU_WIDTH), F32)
    y_prompt, ctx_layers = run_group(ctx, x_prompt, np.zeros(ctx.n_pseudo, np.int32),
                                     lambda l, w: (zeros_state, zeros_state, None, None))
    stack = lambda k, shape: jnp.stack([lay[k].reshape(shape) for lay in ctx_layers], axis=1)
    new_ckv = stack(0, (batch, seq, KV_LORA))
    new_krope = stack(1, (batch, seq, QK_ROPE))
    new_lru_fwd = stack(2, (batch, LRU_WIDTH))
    new_lru_bwd = stack(3, (batch, LRU_WIDTH))

    lat = _Group(dec_batch, dec_seq, latent=True)

    def latent_inputs(l, w_kv_l):
        per_chunk = lambda s: jnp.repeat(s[:, l], lat.n_chunks, axis=0)
        kv_ctx = _kv_up(cache_ckv[:, l].reshape(dec_batch * past, KV_LORA), w_kv_l, past)
        return (per_chunk(state_lru_fwd), per_chunk(state_lru_bwd), kv_ctx,
                cache_krope[:, l].reshape(dec_batch * past, QK_ROPE))

    y_sample, _ = run_group(lat, x_sample, 1 + np.arange(lat.n_pseudo) // lat.n_chunks, latent_inputs)
    return (y_prompt.reshape(batch, seq, D_MODEL), y_sample.reshape(dec_batch, dec_seq, D_MODEL),
            new_ckv, new_krope, new_lru_fwd, new_lru_bwd)
```

```python
import functools

import numpy as np

import jax
import jax.numpy as jnp
from jax import lax
from jax.experimental import pallas as pl
from jax.experimental.pallas import tpu as pltpu

D_MODEL = 1024
DEPTH = 2
GRID_W = 64
N_HEADS = 8
QK_NOPE = 64
QK_ROPE = 32
QK_DIM = QK_NOPE + QK_ROPE
V_DIM = 64
Q_LORA = 384
KV_LORA = 256
ROPE_THETA = 10000.0
LRU_WIDTH = 1024
LRU_BLOCKS = 8
LRU_BW = LRU_WIDTH // LRU_BLOCKS
CONV_W = 4
LRU_C = 8.0
FF_HIDDEN = 2816
N_MOD = 6
EPS = 1e-6

SUBLANES = 8
LANES = 128
VMEM_LIMIT_BYTES = 56 * 1024 * 1024

C_Q = 0
C_KV = C_Q + Q_LORA
C_LX = C_KV + KV_LORA
C_LG = C_LX + LRU_WIDTH
C_MG = C_LG + LRU_WIDTH
C_KR = C_MG + 2 * D_MODEL
IN_COLS_PADDED = C_KR + LANES
HEAD_LANES = LANES
assert QK_DIM <= HEAD_LANES and QK_NOPE + V_DIM == HEAD_LANES
Q_COLS = N_HEADS * HEAD_LANES
KV_COLS = N_HEADS * HEAD_LANES
ATTN_OUT = N_HEADS * HEAD_LANES

STEPS = 64
TILE_ROWS = SUBLANES * STEPS
PERM_STEPS = 32
PERM_ROWS = SUBLANES * PERM_STEPS
assert STEPS % PERM_STEPS == 0
Q_BLOCK = 256
FFN_STEPS = 64
FFN_CHUNK = 256
LRU_CH = 256
LRU_ROWS = 512
SCAN_UNROLL = 8
MOD_TILE = 1536

F32 = jnp.float32
BF16 = jnp.bfloat16


def _params(n_grid_dims):
    return pltpu.CompilerParams(dimension_semantics=("arbitrary",) * n_grid_dims,
                                vmem_limit_bytes=VMEM_LIMIT_BYTES)


def _resident(shape, layer=None):
    zeros = (0,) * len(shape)
    if layer is None:
        return pl.BlockSpec(shape, lambda *_: zeros, pipeline_mode=pl.Buffered(1))
    return pl.BlockSpec((None,) + tuple(shape), lambda *_: (layer,) + zeros, pipeline_mode=pl.Buffered(1))


def _rms(x, g):
    return x * lax.rsqrt(jnp.mean(x * x, axis=-1, keepdims=True) + EPS) * g


def _dot(a, b):
    return jnp.dot(a, b, preferred_element_type=F32)


def _dot_t(a, b):
    return lax.dot_general(a, b, (((1,), (1,)), ((), ())), preferred_element_type=F32)


def _flat(x3):
    return x3.reshape(x3.shape[0] * x3.shape[1], x3.shape[-1])


def _tiled(x2):
    return x2.reshape(SUBLANES, x2.shape[0] // SUBLANES, x2.shape[-1])


def _sigmoid(x):
    return 0.5 * jnp.tanh(0.5 * x) + 0.5


def _to_time_major():
    r = np.arange(PERM_ROWS)
    perm = np.zeros((PERM_ROWS, PERM_ROWS), np.float32)
    perm[r, (r % SUBLANES) * PERM_STEPS + r // SUBLANES] = 1.0
    return perm


def _time_major_rows(perm, x3):
    parts = [_dot(perm, _flat(x3[:, t:t + PERM_STEPS, :])).astype(BF16) for t in range(0, x3.shape[1], PERM_STEPS)]
    return jnp.concatenate(parts, axis=0)


def _seq_major_tile(perm_t, x2):
    parts = [_tiled(_dot(perm_t, x2[r:r + PERM_ROWS, :]).astype(BF16)) for r in range(0, x2.shape[0], PERM_ROWS)]
    return jnp.concatenate(parts, axis=1)


class _Group:
    def __init__(self, batch, seq_len, latent):
        self.batch, self.seq_len, self.latent = batch, seq_len, latent
        self.tokens = batch * seq_len
        self.n_chunks = max(1, SUBLANES // batch)
        self.n_pseudo = batch * self.n_chunks
        assert self.n_chunks in (1, 2) and self.n_pseudo % SUBLANES == 0
        self.chunk_len = seq_len // self.n_chunks
        assert self.chunk_len % STEPS == 0 and seq_len % Q_BLOCK == 0
        self.lru_groups = self.n_pseudo // SUBLANES
        self.tiles_per_chunk = self.chunk_len // STEPS
        self.grid = (self.lru_groups, self.tiles_per_chunk)

    def seq_major(self, width, steps=STEPS):
        return pl.BlockSpec((SUBLANES, steps, width), lambda g, i: (g, i, 0))

    def time_major(self, width):
        return pl.BlockSpec((TILE_ROWS, width), lambda g, i: (g * self.tiles_per_chunk + i, 0))

    def per_pseudo(self, width, layer):
        return pl.BlockSpec((None, SUBLANES, 1, width), lambda g, i: (layer, g, 0, 0))

    def by_position(self, width):
        return pl.BlockSpec((SUBLANES, STEPS, width), lambda g, i: (0, i, 0))

    def shape3(self, width, dtype):
        return jax.ShapeDtypeStruct((self.n_pseudo, self.chunk_len, width), dtype)

    def shape_tm(self, width, dtype):
        return jax.ShapeDtypeStruct((self.lru_groups * self.chunk_len * SUBLANES, width), dtype)


def _mod_kernel(c_ref, w_ref, b_ref, o_ref):
    c = c_ref[...]
    a = (c * _sigmoid(c)).astype(BF16)
    o_ref[...] = _dot(a, w_ref[...].astype(BF16)) + b_ref[...]


def _modulation(cond, w_mod, b_mod):
    n = N_MOD * D_MODEL
    rows = cond.shape[0]
    return pl.pallas_call(
        _mod_kernel,
        grid=(DEPTH, n // MOD_TILE),
        in_specs=[pl.BlockSpec((rows, D_MODEL), lambda l, j: (0, 0)),
                  pl.BlockSpec((None, D_MODEL, MOD_TILE), lambda l, j: (l, 0, j)),
                  pl.BlockSpec((None, 1, MOD_TILE), lambda l, j: (l, 0, j))],
        out_specs=pl.BlockSpec((None, rows, MOD_TILE), lambda l, j: (l, 0, j)),
        out_shape=jax.ShapeDtypeStruct((DEPTH, rows, n), F32),
        compiler_params=_params(2),
        name="modulation",
    )(cond, w_mod, b_mod.reshape(DEPTH, 1, n))


def _in_proj_kernel(latent, x_ref, mod_ref, perm_ref, gpre_ref, win_ref, gq_ref, wq_ref, gkv_ref, wkv_ref, *rest):
    if latent:
        cosq_ref, sinq_ref, ktab_ref = rest[:3]
        rest = rest[3:]
    q_ref, ckv_ref, kv_ref, kr_ref, lx_ref, gg_ref, gates_ref = rest
    sh1, sc1 = mod_ref[:, :, 0:D_MODEL], mod_ref[:, :, D_MODEL:2 * D_MODEL]
    hb3 = (_rms(x_ref[...], gpre_ref[...]) * (1.0 + sc1) + sh1).astype(BF16)
    hb = _flat(hb3)

    def proj(lhs, lo, hi):
        return _dot(lhs, win_ref[:, lo:hi])

    qn = _rms(proj(hb, C_Q, C_KV), gq_ref[...]).astype(BF16)
    qf = _dot(qn, wq_ref[...])
    if latent:
        cos, sin = _flat(cosq_ref[...]), _flat(sinq_ref[...])
        for h in range(N_HEADS):
            lanes = slice(h * HEAD_LANES, (h + 1) * HEAD_LANES)
            partner = slice(Q_COLS + h * HEAD_LANES, Q_COLS + (h + 1) * HEAD_LANES)
            q_ref[:, :, lanes] = _tiled((qf[:, lanes] * cos + qf[:, partner] * sin).astype(BF16))
    else:
        q_ref[...] = _tiled((qf * QK_DIM ** -0.5).astype(BF16))

    ckv = _rms(proj(hb, C_KV, C_LX), gkv_ref[...])
    ckv_ref[...] = _tiled(ckv)
    kv_ref[...] = _tiled(_dot(ckv.astype(BF16), wkv_ref[...]).astype(BF16))

    kr = proj(hb, C_KR, IN_COLS_PADDED)
    if latent:
        pr = kr * _flat(ktab_ref[...])
        kr = pr + pltpu.roll(pr, LANES - QK_ROPE, 1)
    kr_ref[...] = _tiled(kr[:, :QK_ROPE])

    gates_ref[...] = _tiled(_sigmoid(proj(hb, C_MG, C_KR)))

    hb_tm = _time_major_rows(perm_ref[...], hb3)
    lx_ref[...] = proj(hb_tm, C_LX, C_LG)
    gg_ref[...] = jax.nn.gelu(proj(hb_tm, C_LG, C_MG))


def _in_proj(grp, l, x, mod, perm, g_pre, w_in, g_q, w_q, g_kv, w_kv, rope_tabs):
    q_cols = w_q.shape[-1] if grp.latent else Q_COLS
    in_specs = [grp.seq_major(D_MODEL), grp.per_pseudo(N_MOD * D_MODEL, l), _resident(perm.shape),
                _resident((1, D_MODEL), l), _resident(w_in.shape[1:], l), _resident((1, Q_LORA), l),
                _resident((Q_LORA, q_cols), l), _resident((1, KV_LORA), l), _resident(w_kv.shape[1:], l)]
    args = [x, mod, perm, g_pre, w_in, g_q, w_q, g_kv, w_kv]
    if grp.latent:
        in_specs += [grp.by_position(HEAD_LANES), grp.by_position(HEAD_LANES), grp.by_position(LANES)]
        args += list(rope_tabs)
    return pl.pallas_call(
        functools.partial(_in_proj_kernel, grp.latent),
        grid=grp.grid,
        in_specs=in_specs,
        out_specs=[grp.seq_major(Q_COLS), grp.seq_major(KV_LORA), grp.seq_major(KV_COLS), grp.seq_major(QK_ROPE),
                   grp.time_major(LRU_WIDTH), grp.time_major(LRU_WIDTH), grp.seq_major(2 * D_MODEL)],
        out_shape=[grp.shape3(Q_COLS, BF16), grp.shape3(KV_LORA, F32), grp.shape3(KV_COLS, BF16),
                   grp.shape3(QK_ROPE, F32), grp.shape_tm(LRU_WIDTH, F32), grp.shape_tm(LRU_WIDTH, F32),
                   grp.shape3(2 * D_MODEL, F32)],
        compiler_params=_params(2),
        name="in_proj",
    )(*args)


def _kv_up_kernel(c_ref, w_ref, o_ref):
    o_ref[...] = _dot(c_ref[...].astype(BF16), w_ref[...]).astype(BF16)


def _kv_up(l, ckv, w_kv, rows_per_step):
    n = ckv.shape[0]
    return pl.pallas_call(
        _kv_up_kernel,
        grid=(n // rows_per_step,),
        in_specs=[pl.BlockSpec((rows_per_step, KV_LORA), lambda i: (i, 0)), _resident(w_kv.shape[1:], l)],
        out_specs=pl.BlockSpec((rows_per_step, KV_COLS), lambda i: (i, 0)),
        out_shape=jax.ShapeDtypeStruct((n, KV_COLS), BF16),
        compiler_params=_params(1),
        name="kv_up_ctx",
    )(ckv, w_kv)


def _attn_kernel(with_ctx, q_ref, kv_ref, kr_ref, place_ref, *rest):
    if with_ctx:
        kvc_ref, krc_ref, o_ref = rest
    else:
        (o_ref,) = rest
    is_nope = lax.broadcasted_iota(jnp.int32, (1, HEAD_LANES), 1) < QK_NOPE

    def keys_of(kv, kr):
        kr_placed = _dot(kr[...].astype(BF16), place_ref[...]).astype(BF16)
        return lambda h: jnp.where(is_nope, kv[:, h * HEAD_LANES:(h + 1) * HEAD_LANES], kr_placed)

    keys = keys_of(kv_ref, kr_ref)
    if with_ctx:
        keys_ctx = keys_of(kvc_ref, krc_ref)
    for h in range(N_HEADS):
        lanes = slice(h * HEAD_LANES, (h + 1) * HEAD_LANES)
        q = q_ref[:, lanes]
        s = _dot_t(q, keys(h))
        m = jnp.max(s, axis=-1, keepdims=True)
        if with_ctx:
            sc = _dot_t(q, keys_ctx(h))
            m = jnp.maximum(m, jnp.max(sc, axis=-1, keepdims=True))
        p = jnp.exp(s - m)
        den = jnp.sum(p, axis=-1, keepdims=True)
        o = _dot(p.astype(BF16), kv_ref[:, lanes])
        if with_ctx:
            pc = jnp.exp(sc - m)
            den = den + jnp.sum(pc, axis=-1, keepdims=True)
            o = o + _dot(pc.astype(BF16), kvc_ref[:, lanes])
        o_ref[:, lanes] = (o / den).astype(BF16)


def _attention(grp, q, kv, kr, kv_ctx=None, kr_ctx=None):
    with_ctx = kv_ctx is not None
    nq = grp.seq_len // Q_BLOCK
    seq = lambda rows, width: pl.BlockSpec((rows, width), lambda b, i: (b, 0))
    place = np.zeros((QK_ROPE, HEAD_LANES), np.float32)
    place[np.arange(QK_ROPE), QK_NOPE + np.arange(QK_ROPE)] = 1.0
    in_specs = [pl.BlockSpec((Q_BLOCK, Q_COLS), lambda b, i: (b * nq + i, 0)),
                seq(grp.seq_len, KV_COLS), seq(grp.seq_len, QK_ROPE), _resident(place.shape)]
    args = [q, kv, kr, jnp.asarray(place, BF16)]
    if with_ctx:
        past = kv_ctx.shape[0] // grp.batch
        in_specs += [seq(past, KV_COLS), seq(past, QK_ROPE)]
        args += [kv_ctx, kr_ctx]
    return pl.pallas_call(
        functools.partial(_attn_kernel, with_ctx),
        grid=(grp.batch, nq),
        in_specs=in_specs,
        out_specs=pl.BlockSpec((Q_BLOCK, ATTN_OUT), lambda b, i: (b * nq + i, 0)),
        out_shape=jax.ShapeDtypeStruct((grp.tokens, ATTN_OUT), BF16),
        compiler_params=_params(2),
        name="attention",
    )(*args)


def _softplus(z):
    return jnp.maximum(z, 0.0) + jnp.log1p(jnp.exp(-jnp.abs(z)))


def _lru_kernel(n_chunks, chunk_len, with_state, x_ref, gg_ref, cw_ref, cb_ref, wrg_ref, brg_ref,
                wig_ref, big_ref, lam_ref, hf0_ref, hb0_ref, *rest):
    if with_state:
        s_ref, hf_ref, hb_ref, xpad, a_f, u_f, a_b, u_b = rest
    else:
        s_ref, xpad, a_f, u_f, a_b, u_b = rest
    rows = chunk_len * SUBLANES
    halo = (CONV_W // 2) * SUBLANES
    sub = lax.broadcasted_iota(jnp.int32, (SUBLANES, LRU_CH), 0)
    first_chunk = (sub % n_chunks) == 0
    last_chunk = (sub % n_chunks) == n_chunks - 1

    def from_prev(v):
        return jnp.where(first_chunk, 0.0, pltpu.roll(v, 1, 0))

    def from_next(v):
        return jnp.where(last_chunk, 0.0, pltpu.roll(v, SUBLANES - 1, 0))

    xpad[halo:halo + rows, :] = x_ref[...]
    xpad[0:SUBLANES, :] = from_prev(x_ref[rows - 2 * SUBLANES:rows - SUBLANES, :])
    xpad[SUBLANES:halo, :] = from_prev(x_ref[rows - SUBLANES:rows, :])
    xpad[halo + rows:halo + rows + SUBLANES, :] = from_next(x_ref[0:SUBLANES, :])

    def gates(c, carry):
        r0 = pl.multiple_of(c * LRU_ROWS, LRU_ROWS)
        xc = cb_ref[...]
        for k in range(CONV_W):
            xc = xc + xpad[pl.ds(r0 + k * SUBLANES, LRU_ROWS), :] * cw_ref[k:k + 1, :]
        xcb = xc.astype(BF16)
        for d, (a_ref, u_ref) in enumerate(((a_f, u_f), (a_b, u_b))):
            for n in range(LRU_CH // LRU_BW):
                sl = slice(n * LRU_BW, (n + 1) * LRU_BW)
                t_r = jnp.tanh(_dot(xcb[:, sl], wrg_ref[d, n]) + brg_ref[d:d + 1, sl])
                t_i = jnp.tanh(_dot(xcb[:, sl], wig_ref[d, n]) + big_ref[d:d + 1, sl])
                half_c = (-0.5 * LRU_C) * _softplus(-lam_ref[d:d + 1, sl])
                log_a = half_c * t_r + half_c
                a = jnp.exp(log_a)
                a_ref[pl.ds(r0, LRU_ROWS), sl] = a
                one_minus_a2 = -jnp.tanh(log_a) * (a * a + 1.0)
                root = jnp.where(one_minus_a2 > 0.0, one_minus_a2 * lax.rsqrt(one_minus_a2), 0.0)
                u_ref[pl.ds(r0, LRU_ROWS), sl] = root * (0.5 * t_i + 0.5) * xc[:, sl]
        return carry

    lax.fori_loop(0, rows // LRU_ROWS, gates, 0)

    def scan(hf, hb, store):
        def step(t, carry):
            hf, hb = carry
            rf = pl.multiple_of(t * SUBLANES, SUBLANES)
            rb = pl.multiple_of((chunk_len - 1 - t) * SUBLANES, SUBLANES)
            hf = a_f[pl.ds(rf, SUBLANES), :] * hf + u_f[pl.ds(rf, SUBLANES), :]
            hb = a_b[pl.ds(rb, SUBLANES), :] * hb + u_b[pl.ds(rb, SUBLANES), :]
            if store:
                u_f[pl.ds(rf, SUBLANES), :] = hf
                u_b[pl.ds(rb, SUBLANES), :] = hb
            return hf, hb
        return lax.fori_loop(0, chunk_len, step, (hf, hb), unroll=SCAN_UNROLL)

    hf, hb = hf0_ref[...], hb0_ref[...]
    if n_chunks > 1:
        ef, eb = scan(hf, hb, False)
        hf = jnp.where(first_chunk, hf, pltpu.roll(ef, 1, 0))
        hb = jnp.where(last_chunk, hb, pltpu.roll(eb, SUBLANES - 1, 0))
    hf, hb = scan(hf, hb, True)
    if with_state:
        hf_ref[...] = hf
        hb_ref[...] = hb

    def combine(c, carry):
        r = pl.ds(pl.multiple_of(c * LRU_ROWS, LRU_ROWS), LRU_ROWS)
        s_ref[r, :] = ((u_f[r, :] + u_b[r, :]) * gg_ref[r, :]).astype(BF16)
        return carry

    lax.fori_loop(0, rows // LRU_ROWS, combine, 0)


def _rg_lru(grp, l, lx, gg, conv_w, conv_b, w_rg, b_rg, w_ig, b_ig, lam, hf0, hb0, with_state):
    assert grp.n_chunks == 1 or not with_state
    rows = grp.chunk_len * SUBLANES
    nb = LRU_CH // LRU_BW
    tile = pl.BlockSpec((rows, LRU_CH), lambda g, c: (g, c))
    chan = lambda r: pl.BlockSpec((None, r, LRU_CH), lambda g, c: (l, 0, c))
    wblk = pl.BlockSpec((None, 2, nb, LRU_BW, LRU_BW), lambda g, c: (l, 0, c, 0, 0))
    state = pl.BlockSpec((SUBLANES, LRU_CH), lambda g, c: (g, c))
    out_specs = [tile]
    out_shape = [grp.shape_tm(LRU_WIDTH, BF16)]
    if with_state:
        out_specs += [state, state]
        out_shape += [jax.ShapeDtypeStruct((grp.n_pseudo, LRU_WIDTH), F32)] * 2
    scratch = [pltpu.VMEM((rows + (CONV_W - 1) * SUBLANES, LRU_CH), F32)] + [pltpu.VMEM((rows, LRU_CH), F32)] * 4
    return pl.pallas_call(
        functools.partial(_lru_kernel, grp.n_chunks, grp.chunk_len, with_state),
        grid=(grp.lru_groups, LRU_WIDTH // LRU_CH),
        in_specs=[tile, tile, chan(CONV_W), chan(1), wblk, chan(2), wblk, chan(2), chan(2), state, state],
        out_specs=out_specs,
        out_shape=out_shape,
        scratch_shapes=scratch,
        compiler_params=_params(2),
        name="rg_lru",
    )(lx, gg, conv_w, conv_b, w_rg, b_rg, w_ig, b_ig, lam, hf0, hb0)


def _merge_kernel(o_ref, s_ref, gates_ref, x_ref, mod_ref, perm_ref, gpost_ref, wom_ref, wol_ref, wout_ref, y_ref):
    y_mla = _dot(_flat(o_ref[...]), wom_ref[...])
    s = _flat(_seq_major_tile(perm_ref[...], s_ref[...]))
    y_lru = _dot(s, wol_ref[...])
    gates = _flat(gates_ref[...])
    z = gates[:, :D_MODEL] * y_mla + gates[:, D_MODEL:] * y_lru
    y = _dot(z.astype(BF16), wout_ref[...])
    g1 = mod_ref[:, :, 2 * D_MODEL:3 * D_MODEL]
    y_ref[...] = x_ref[...] + g1 * _tiled(_rms(y, gpost_ref[...]))


def _merge(grp, l, o, s, gates, x, mod, perm_t, g_post, w_o_mla, w_o_lru, w_out):
    return pl.pallas_call(
        _merge_kernel,
        grid=grp.grid,
        in_specs=[grp.seq_major(ATTN_OUT), grp.time_major(LRU_WIDTH), grp.seq_major(2 * D_MODEL),
                  grp.seq_major(D_MODEL), grp.per_pseudo(N_MOD * D_MODEL, l), _resident(perm_t.shape),
                  _resident((1, D_MODEL), l), _resident(w_o_mla.shape[1:], l), _resident(w_o_lru.shape[1:], l),
                  _resident(w_out.shape[1:], l)],
        out_specs=grp.seq_major(D_MODEL),
        out_shape=grp.shape3(D_MODEL, F32),
        compiler_params=_params(2),
        name="merge",
    )(o, s, gates, x, mod, perm_t, g_post, w_o_mla, w_o_lru, w_out)


def _ffn_kernel(x_ref, mod_ref, gpre_ref, gpost_ref, w1_ref, w2_ref, y_ref, act_ref):
    x = x_ref[...]
    sh2, sc2, g2 = (mod_ref[:, :, k * D_MODEL:(k + 1) * D_MODEL] for k in (3, 4, 5))
    hb = _flat((_rms(x, gpre_ref[...]) * (1.0 + sc2) + sh2).astype(BF16))
    for c in range(FF_HIDDEN // FFN_CHUNK):
        lo = c * FFN_CHUNK
        g = _dot(hb, w1_ref[:, lo:lo + FFN_CHUNK])
        u = _dot(hb, w1_ref[:, FF_HIDDEN + lo:FF_HIDDEN + lo + FFN_CHUNK])
        act_ref[:, lo:lo + FFN_CHUNK] = (g * _sigmoid(g) * u).astype(BF16)
    y = _dot(act_ref[...], w2_ref[...])
    y_ref[...] = x + g2 * _tiled(_rms(y, gpost_ref[...]))


def _ffn(grp, l, x, mod, g_pre, g_post, w1, w2):
    assert grp.chunk_len % FFN_STEPS == 0
    return pl.pallas_call(
        _ffn_kernel,
        grid=(grp.lru_groups, grp.chunk_len // FFN_STEPS),
        in_specs=[grp.seq_major(D_MODEL, FFN_STEPS), grp.per_pseudo(N_MOD * D_MODEL, l),
                  _resident((1, D_MODEL), l), _resident((1, D_MODEL), l), _resident(w1.shape[1:], l),
                  _resident(w2.shape[1:], l)],
        out_specs=grp.seq_major(D_MODEL, FFN_STEPS),
        out_shape=grp.shape3(D_MODEL, F32),
        scratch_shapes=[pltpu.VMEM((SUBLANES * FFN_STEPS, FF_HIDDEN), BF16)],
        compiler_params=_params(2),
        name="ffn",
    )(x, mod, g_pre, g_post, w1, w2)


def _rot_partner(w):
    half = QK_ROPE // 2
    return jnp.concatenate([-w[..., half:], w[..., :half]], axis=-1)


def _rope_tables(grp):
    n_tokens = grp.seq_len
    rows = n_tokens // GRID_W
    row = jnp.repeat(jnp.arange(rows, dtype=F32), GRID_W)
    col = jnp.tile(jnp.arange(GRID_W, dtype=F32), rows)
    n_freq = QK_ROPE // 4
    inv = ROPE_THETA ** (-jnp.arange(n_freq, dtype=F32) / n_freq)
    ang = jnp.concatenate([row[:, None] * inv, col[:, None] * inv], axis=-1)
    cos, sin = jnp.cos(ang), jnp.sin(ang)
    cos2, sin2 = jnp.concatenate([cos, cos], -1), jnp.concatenate([sin, sin], -1)
    ktab = jnp.concatenate([cos2, sin2, jnp.zeros((n_tokens, LANES - 2 * QK_ROPE), F32)], -1)
    scale = QK_DIM ** -0.5
    tail = jnp.zeros((n_tokens, HEAD_LANES - QK_DIM), F32)
    qcos = jnp.concatenate([jnp.ones((n_tokens, QK_NOPE), F32), cos2, tail], -1) * scale
    qsin = jnp.concatenate([jnp.zeros((n_tokens, QK_NOPE), F32), sin2, tail], -1) * scale

    def by_pseudo(tab):
        tab = tab.reshape(grp.n_chunks, grp.chunk_len, tab.shape[-1])
        return jnp.tile(tab, (SUBLANES // grp.n_chunks, 1, 1))

    return by_pseudo(qcos), by_pseudo(qsin), by_pseudo(ktab)


def kernel(x_prompt, x_sample, cache_ckv, cache_krope, state_lru_fwd, state_lru_bwd, c, c_ctx, w_mod, b_mod, g_pre_mix, g_post_mix, g_pre_ffn, g_post_ffn, w_in, g_q, w_q_up, g_kv, w_kv_up, w_o_mla, conv_w, conv_b, w_rg, b_rg, w_ig, b_ig, lru_lambda, w_o_lru, w_out, w_ffn_in, w_ffn_out):
    batch, seq = x_prompt.shape[:2]
    dec_batch, dec_seq = x_sample.shape[:2]
    past = cache_ckv.shape[2]

    off_kr = Q_LORA + KV_LORA
    w_kr = w_in[:, :, off_kr:off_kr + QK_ROPE]
    w_in_r = jnp.concatenate(
        [w_in[:, :, :off_kr], w_in[:, :, off_kr + QK_ROPE:], w_kr, _rot_partner(w_kr),
         jnp.zeros((DEPTH, D_MODEL, LANES - 2 * QK_ROPE), F32)], axis=-1).astype(BF16)
    wq = w_q_up.reshape(DEPTH, Q_LORA, N_HEADS, QK_DIM)
    q_tail = jnp.zeros((DEPTH, Q_LORA, N_HEADS, HEAD_LANES - QK_DIM), F32)
    q_main = jnp.concatenate([wq, q_tail], axis=-1)
    q_partner = jnp.concatenate([jnp.zeros_like(wq[..., :QK_NOPE]), _rot_partner(wq[..., QK_NOPE:]), q_tail], axis=-1)
    w_q_r = jnp.concatenate([q_main.reshape(DEPTH, Q_LORA, Q_COLS),
                             q_partner.reshape(DEPTH, Q_LORA, Q_COLS)], axis=-1).astype(BF16)
    w_kv_r = w_kv_up.astype(BF16)
    wo = w_o_mla.reshape(DEPTH, N_HEADS, V_DIM, D_MODEL)
    w_o_mla_b = jnp.concatenate([jnp.zeros((DEPTH, N_HEADS, HEAD_LANES - V_DIM, D_MODEL), F32), wo],
                                axis=2).reshape(DEPTH, ATTN_OUT, D_MODEL).astype(BF16)
    w_o_lru_b, w_out_b = w_o_lru.astype(BF16), w_out.astype(BF16)
    w_rg_b, w_ig_b = (0.5 * w_rg).astype(BF16), (0.5 * w_ig).astype(BF16)
    b_rg, b_ig = 0.5 * b_rg, 0.5 * b_ig
    w1_b, w2_b = w_ffn_in.astype(BF16), w_ffn_out.astype(BF16)
    row = lambda v: v.reshape(DEPTH, 1, -1)
    g_pre_mix, g_post_mix, g_pre_ffn, g_post_ffn = map(row, (g_pre_mix, g_post_mix, g_pre_ffn, g_post_ffn))
    g_q, g_kv, conv_b = row(g_q), row(g_kv), row(conv_b)
    perm = _to_time_major()
    perm_b, perm_t_b = jnp.asarray(perm, BF16), jnp.asarray(perm.T, BF16)

    mod_all = _modulation(jnp.concatenate([c_ctx[None, :], c], axis=0), w_mod, b_mod)

    def run_group(grp, x, mod_rows, layer_inputs):
        x = x.reshape(grp.n_pseudo, grp.chunk_len, D_MODEL)
        rope_tabs = _rope_tables(grp) if grp.latent else None
        flat = lambda a: a.reshape(grp.tokens, a.shape[-1])
        mod = mod_all[:, mod_rows][:, :, None, :]
        per_layer = []
        for l in range(DEPTH):
            q, ckv, kv, kr, lx, gg, gates = _in_proj(grp, l, x, mod, perm_b, g_pre_mix, w_in_r, g_q, w_q_r,
                                                     g_kv, w_kv_r, rope_tabs)
            hf0, hb0, kv_ctx, kr_ctx = layer_inputs(l)
            o = _attention(grp, flat(q), flat(kv), flat(kr), kv_ctx, kr_ctx)
            lru = _rg_lru(grp, l, lx, gg, conv_w, conv_b, w_rg_b, b_rg, w_ig_b, b_ig, lru_lambda,
                          hf0, hb0, not grp.latent)
            o = o.reshape(grp.n_pseudo, grp.chunk_len, ATTN_OUT)
            x = _merge(grp, l, o, lru[0], gates, x, mod, perm_t_b, g_post_mix, w_o_mla_b, w_o_lru_b, w_out_b)
            x = _ffn(grp, l, x, mod, g_pre_ffn, g_post_ffn, w1_b, w2_b)
            per_layer.append((ckv, kr) + tuple(lru[1:]))
        return x, per_layer

    ctx = _Group(batch, seq, latent=False)
    zeros_state = jnp.zeros((batch, LRU_WIDTH), F32)
    y_prompt, ctx_layers = run_group(ctx, x_prompt, np.zeros(ctx.n_pseudo, np.int32),
                                     lambda l: (zeros_state, zeros_state, None, None))
    stack = lambda k, shape: jnp.stack([lay[k].reshape(shape) for lay in ctx_layers], axis=1)
    new_ckv = stack(0, (batch, seq, KV_LORA))
    new_krope = stack(1, (batch, seq, QK_ROPE))
    new_lru_fwd = stack(2, (batch, LRU_WIDTH))
    new_lru_bwd = stack(3, (batch, LRU_WIDTH))

    lat = _Group(dec_batch, dec_seq, latent=True)

    def latent_inputs(l):
        per_chunk = lambda s: jnp.repeat(s[:, l], lat.n_chunks, axis=0)
        kv_ctx = _kv_up(l, cache_ckv[:, l].reshape(dec_batch * past, KV_LORA), w_kv_r, past)
        return (per_chunk(state_lru_fwd), per_chunk(state_lru_bwd), kv_ctx,
                cache_krope[:, l].reshape(dec_batch * past, QK_ROPE))

    y_sample, _ = run_group(lat, x_sample, 1 + np.arange(lat.n_pseudo) // lat.n_chunks, latent_inputs)
    return (y_prompt.reshape(batch, seq, D_MODEL), y_sample.reshape(dec_batch, dec_seq, D_MODEL),
            new_ckv, new_krope, new_lru_fwd, new_lru_bwd)
```

```python
import functools

import numpy as np

import jax
import jax.numpy as jnp
from jax import lax
from jax.experimental import pallas as pl
from jax.experimental.pallas import tpu as pltpu

D_MODEL = 1024
DEPTH = 2
GRID_W = 64
N_HEADS = 8
QK_NOPE = 64
QK_ROPE = 32
QK_DIM = QK_NOPE + QK_ROPE
V_DIM = 64
Q_LORA = 384
KV_LORA = 256
ROPE_THETA = 10000.0
LRU_WIDTH = 1024
LRU_BLOCKS = 8
LRU_BW = LRU_WIDTH // LRU_BLOCKS
CONV_W = 4
LRU_C = 8.0
FF_HIDDEN = 2816
N_MOD = 6
EPS = 1e-6

SUBLANES = 8
LANES = 128
VMEM_LIMIT_BYTES = 56 * 1024 * 1024

C_Q = 0
C_KV = C_Q + Q_LORA
C_LX = C_KV + KV_LORA
C_LG = C_LX + LRU_WIDTH
C_MG = C_LG + LRU_WIDTH
C_KR = C_MG + 2 * D_MODEL
IN_COLS_PADDED = C_KR + LANES
HEAD_LANES = LANES
assert QK_DIM <= HEAD_LANES and QK_NOPE + V_DIM == HEAD_LANES
Q_COLS = N_HEADS * HEAD_LANES
KV_COLS = N_HEADS * HEAD_LANES
ATTN_OUT = N_HEADS * HEAD_LANES

STEPS = 64
TILE_ROWS = SUBLANES * STEPS
PERM_STEPS = 32
PERM_ROWS = SUBLANES * PERM_STEPS
assert STEPS % PERM_STEPS == 0
Q_BLOCK = 256
FFN_STEPS = 64
FFN_CHUNK = 256
LRU_CH = 256
LRU_ROWS = 512
SCAN_BLOCK = 4
SCAN_UNROLL = 2
MOD_TILE = 1536

F32 = jnp.float32
BF16 = jnp.bfloat16


def _params(n_grid_dims):
    return pltpu.CompilerParams(dimension_semantics=("arbitrary",) * n_grid_dims,
                                vmem_limit_bytes=VMEM_LIMIT_BYTES)


def _resident(shape, layer=None):
    zeros = (0,) * len(shape)
    if layer is None:
        return pl.BlockSpec(shape, lambda *_: zeros, pipeline_mode=pl.Buffered(1))
    return pl.BlockSpec((None,) + tuple(shape), lambda *_: (layer,) + zeros, pipeline_mode=pl.Buffered(1))


def _rms(x, g):
    return x * lax.rsqrt(jnp.mean(x * x, axis=-1, keepdims=True) + EPS) * g


def _dot(a, b):
    return jnp.dot(a, b, preferred_element_type=F32)


def _dot_t(a, b):
    return lax.dot_general(a, b, (((1,), (1,)), ((), ())), preferred_element_type=F32)


def _flat(x3):
    return x3.reshape(x3.shape[0] * x3.shape[1], x3.shape[-1])


def _tiled(x2):
    return x2.reshape(SUBLANES, x2.shape[0] // SUBLANES, x2.shape[-1])


def _sigmoid(x):
    return 0.5 * jnp.tanh(0.5 * x) + 0.5


def _to_time_major():
    r = np.arange(PERM_ROWS)
    perm = np.zeros((PERM_ROWS, PERM_ROWS), np.float32)
    perm[r, (r % SUBLANES) * PERM_STEPS + r // SUBLANES] = 1.0
    return perm


def _time_major_rows(perm, x3):
    parts = [_dot(perm, _flat(x3[:, t:t + PERM_STEPS, :])).astype(BF16) for t in range(0, x3.shape[1], PERM_STEPS)]
    return jnp.concatenate(parts, axis=0)


def _seq_major_tile(perm_t, x2):
    parts = [_tiled(_dot(perm_t, x2[r:r + PERM_ROWS, :]).astype(BF16)) for r in range(0, x2.shape[0], PERM_ROWS)]
    return jnp.concatenate(parts, axis=1)


class _Group:
    def __init__(self, batch, seq_len, latent):
        self.batch, self.seq_len, self.latent = batch, seq_len, latent
        self.tokens = batch * seq_len
        self.n_chunks = max(1, SUBLANES // batch)
        self.n_pseudo = batch * self.n_chunks
        assert self.n_chunks in (1, 2) and self.n_pseudo % SUBLANES == 0
        self.chunk_len = seq_len // self.n_chunks
        assert self.chunk_len % STEPS == 0 and seq_len % Q_BLOCK == 0
        self.lru_groups = self.n_pseudo // SUBLANES
        self.tiles_per_chunk = self.chunk_len // STEPS
        self.grid = (self.lru_groups, self.tiles_per_chunk)

    def seq_major(self, width, steps=STEPS):
        return pl.BlockSpec((SUBLANES, steps, width), lambda g, i: (g, i, 0))

    def time_major(self, width):
        return pl.BlockSpec((TILE_ROWS, width), lambda g, i: (g * self.tiles_per_chunk + i, 0))

    def per_pseudo(self, width, layer):
        return pl.BlockSpec((None, SUBLANES, 1, width), lambda g, i: (layer, g, 0, 0))

    def by_position(self, width):
        return pl.BlockSpec((SUBLANES, STEPS, width), lambda g, i: (0, i, 0))

    def shape3(self, width, dtype):
        return jax.ShapeDtypeStruct((self.n_pseudo, self.chunk_len, width), dtype)

    def shape_tm(self, width, dtype):
        return jax.ShapeDtypeStruct((self.lru_groups * self.chunk_len * SUBLANES, width), dtype)


def _mod_kernel(c_ref, w_ref, b_ref, o_ref):
    c = c_ref[...]
    a = (c * _sigmoid(c)).astype(BF16)
    o_ref[...] = _dot(a, w_ref[...].astype(BF16)) + b_ref[...]


def _modulation(cond, w_mod, b_mod):
    n = N_MOD * D_MODEL
    rows = cond.shape[0]
    return pl.pallas_call(
        _mod_kernel,
        grid=(DEPTH, n // MOD_TILE),
        in_specs=[pl.BlockSpec((rows, D_MODEL), lambda l, j: (0, 0)),
                  pl.BlockSpec((None, D_MODEL, MOD_TILE), lambda l, j: (l, 0, j)),
                  pl.BlockSpec((None, 1, MOD_TILE), lambda l, j: (l, 0, j))],
        out_specs=pl.BlockSpec((None, rows, MOD_TILE), lambda l, j: (l, 0, j)),
        out_shape=jax.ShapeDtypeStruct((DEPTH, rows, n), F32),
        compiler_params=_params(2),
        name="modulation",
    )(cond, w_mod, b_mod.reshape(DEPTH, 1, n))


def _in_proj_kernel(latent, x_ref, mod_ref, perm_ref, gpre_ref, win_ref, gq_ref, wq_ref, gkv_ref, wkv_ref, *rest):
    if latent:
        cosq_ref, sinq_ref, ktab_ref = rest[:3]
        rest = rest[3:]
    q_ref, ckv_ref, kv_ref, kr_ref, lx_ref, gg_ref, gates_ref = rest
    sh1, sc1 = mod_ref[:, :, 0:D_MODEL], mod_ref[:, :, D_MODEL:2 * D_MODEL]
    hb3 = (_rms(x_ref[...], gpre_ref[...]) * (1.0 + sc1) + sh1).astype(BF16)
    hb = _flat(hb3)

    def proj(lhs, lo, hi):
        return _dot(lhs, win_ref[:, lo:hi])

    qn = _rms(proj(hb, C_Q, C_KV), gq_ref[...]).astype(BF16)
    qf = _dot(qn, wq_ref[...])
    if latent:
        cos, sin = _flat(cosq_ref[...]), _flat(sinq_ref[...])
        for h in range(N_HEADS):
            lanes = slice(h * HEAD_LANES, (h + 1) * HEAD_LANES)
            partner = slice(Q_COLS + h * HEAD_LANES, Q_COLS + (h + 1) * HEAD_LANES)
            q_ref[:, :, lanes] = _tiled((qf[:, lanes] * cos + qf[:, partner] * sin).astype(BF16))
    else:
        q_ref[...] = _tiled((qf * QK_DIM ** -0.5).astype(BF16))

    ckv = _rms(proj(hb, C_KV, C_LX), gkv_ref[...])
    ckv_ref[...] = _tiled(ckv)
    kv_ref[...] = _tiled(_dot(ckv.astype(BF16), wkv_ref[...]).astype(BF16))

    kr = proj(hb, C_KR, IN_COLS_PADDED)
    if latent:
        pr = kr * _flat(ktab_ref[...])
        kr = pr + pltpu.roll(pr, LANES - QK_ROPE, 1)
    kr_ref[...] = _tiled(kr[:, :QK_ROPE])

    gates_ref[...] = _tiled(_sigmoid(proj(hb, C_MG, C_KR)))

    hb_tm = _time_major_rows(perm_ref[...], hb3)
    lx_ref[...] = proj(hb_tm, C_LX, C_LG)
    gg_ref[...] = jax.nn.gelu(proj(hb_tm, C_LG, C_MG))


def _in_proj(grp, l, x, mod, perm, g_pre, w_in, g_q, w_q, g_kv, w_kv, rope_tabs):
    q_cols = w_q.shape[-1] if grp.latent else Q_COLS
    in_specs = [grp.seq_major(D_MODEL), grp.per_pseudo(N_MOD * D_MODEL, l), _resident(perm.shape),
                _resident((1, D_MODEL), l), _resident(w_in.shape[1:], l), _resident((1, Q_LORA), l),
                _resident((Q_LORA, q_cols), l), _resident((1, KV_LORA), l), _resident(w_kv.shape[1:], l)]
    args = [x, mod, perm, g_pre, w_in, g_q, w_q, g_kv, w_kv]
    if grp.latent:
        in_specs += [grp.by_position(HEAD_LANES), grp.by_position(HEAD_LANES), grp.by_position(LANES)]
        args += list(rope_tabs)
    return pl.pallas_call(
        functools.partial(_in_proj_kernel, grp.latent),
        grid=grp.grid,
        in_specs=in_specs,
        out_specs=[grp.seq_major(Q_COLS), grp.seq_major(KV_LORA), grp.seq_major(KV_COLS), grp.seq_major(QK_ROPE),
                   grp.time_major(LRU_WIDTH), grp.time_major(LRU_WIDTH), grp.seq_major(2 * D_MODEL)],
        out_shape=[grp.shape3(Q_COLS, BF16), grp.shape3(KV_LORA, F32), grp.shape3(KV_COLS, BF16),
                   grp.shape3(QK_ROPE, F32), grp.shape_tm(LRU_WIDTH, F32), grp.shape_tm(LRU_WIDTH, F32),
                   grp.shape3(2 * D_MODEL, F32)],
        compiler_params=_params(2),
        name="in_proj",
    )(*args)


def _kv_up_kernel(c_ref, w_ref, o_ref):
    o_ref[...] = _dot(c_ref[...].astype(BF16), w_ref[...]).astype(BF16)


def _kv_up(l, ckv, w_kv, rows_per_step):
    n = ckv.shape[0]
    return pl.pallas_call(
        _kv_up_kernel,
        grid=(n // rows_per_step,),
        in_specs=[pl.BlockSpec((rows_per_step, KV_LORA), lambda i: (i, 0)), _resident(w_kv.shape[1:], l)],
        out_specs=pl.BlockSpec((rows_per_step, KV_COLS), lambda i: (i, 0)),
        out_shape=jax.ShapeDtypeStruct((n, KV_COLS), BF16),
        compiler_params=_params(1),
        name="kv_up_ctx",
    )(ckv, w_kv)


def _attn_kernel(with_ctx, q_ref, kv_ref, kr_ref, place_ref, *rest):
    if with_ctx:
        kvc_ref, krc_ref, o_ref = rest
    else:
        (o_ref,) = rest
    is_nope = lax.broadcasted_iota(jnp.int32, (1, HEAD_LANES), 1) < QK_NOPE

    def keys_of(kv, kr):
        kr_placed = _dot(kr[...].astype(BF16), place_ref[...]).astype(BF16)
        return lambda h: jnp.where(is_nope, kv[:, h * HEAD_LANES:(h + 1) * HEAD_LANES], kr_placed)

    keys = keys_of(kv_ref, kr_ref)
    if with_ctx:
        keys_ctx = keys_of(kvc_ref, krc_ref)
    for h in range(N_HEADS):
        lanes = slice(h * HEAD_LANES, (h + 1) * HEAD_LANES)
        q = q_ref[:, lanes]
        s = _dot_t(q, keys(h))
        m = jnp.max(s, axis=-1, keepdims=True)
        if with_ctx:
            sc = _dot_t(q, keys_ctx(h))
            m = jnp.maximum(m, jnp.max(sc, axis=-1, keepdims=True))
        p = jnp.exp(s - m)
        den = jnp.sum(p, axis=-1, keepdims=True)
        o = _dot(p.astype(BF16), kv_ref[:, lanes])
        if with_ctx:
            pc = jnp.exp(sc - m)
            den = den + jnp.sum(pc, axis=-1, keepdims=True)
            o = o + _dot(pc.astype(BF16), kvc_ref[:, lanes])
        o_ref[:, lanes] = (o / den).astype(BF16)


def _attention(grp, q, kv, kr, kv_ctx=None, kr_ctx=None):
    with_ctx = kv_ctx is not None
    nq = grp.seq_len // Q_BLOCK
    seq = lambda rows, width: pl.BlockSpec((rows, width), lambda b, i: (b, 0))
    place = np.zeros((QK_ROPE, HEAD_LANES), np.float32)
    place[np.arange(QK_ROPE), QK_NOPE + np.arange(QK_ROPE)] = 1.0
    in_specs = [pl.BlockSpec((Q_BLOCK, Q_COLS), lambda b, i: (b * nq + i, 0)),
                seq(grp.seq_len, KV_COLS), seq(grp.seq_len, QK_ROPE), _resident(place.shape)]
    args = [q, kv, kr, jnp.asarray(place, BF16)]
    if with_ctx:
        past = kv_ctx.shape[0] // grp.batch
        in_specs += [seq(past, KV_COLS), seq(past, QK_ROPE)]
        args += [kv_ctx, kr_ctx]
    return pl.pallas_call(
        functools.partial(_attn_kernel, with_ctx),
        grid=(grp.batch, nq),
        in_specs=in_specs,
        out_specs=pl.BlockSpec((Q_BLOCK, ATTN_OUT), lambda b, i: (b * nq + i, 0)),
        out_shape=jax.ShapeDtypeStruct((grp.tokens, ATTN_OUT), BF16),
        compiler_params=_params(2),
        name="attention",
    )(*args)


def _softplus(z):
    return jnp.maximum(z, 0.0) + jnp.log1p(jnp.exp(-jnp.abs(z)))


def _lru_kernel(n_chunks, chunk_len, with_state, x_ref, gg_ref, cw_ref, cb_ref, wrg_ref, brg_ref,
                wig_ref, big_ref, lam_ref, hf0_ref, hb0_ref, *rest):
    if with_state:
        s_ref, hf_ref, hb_ref, xpad, a_f, u_f, a_b, u_b = rest
    else:
        s_ref, xpad, a_f, u_f, a_b, u_b = rest
    rows = chunk_len * SUBLANES
    halo = (CONV_W // 2) * SUBLANES
    sub = lax.broadcasted_iota(jnp.int32, (SUBLANES, LRU_CH), 0)
    first_chunk = (sub % n_chunks) == 0
    last_chunk = (sub % n_chunks) == n_chunks - 1

    def from_prev(v):
        return jnp.where(first_chunk, 0.0, pltpu.roll(v, 1, 0))

    def from_next(v):
        return jnp.where(last_chunk, 0.0, pltpu.roll(v, SUBLANES - 1, 0))

    xpad[halo:halo + rows, :] = x_ref[...]
    xpad[0:SUBLANES, :] = from_prev(x_ref[rows - 2 * SUBLANES:rows - SUBLANES, :])
    xpad[SUBLANES:halo, :] = from_prev(x_ref[rows - SUBLANES:rows, :])
    xpad[halo + rows:halo + rows + SUBLANES, :] = from_next(x_ref[0:SUBLANES, :])

    def gates(c, carry):
        r0 = pl.multiple_of(c * LRU_ROWS, LRU_ROWS)
        xc = cb_ref[...]
        for k in range(CONV_W):
            xc = xc + xpad[pl.ds(r0 + k * SUBLANES, LRU_ROWS), :] * cw_ref[k:k + 1, :]
        xcb = xc.astype(BF16)
        for d, (a_ref, u_ref) in enumerate(((a_f, u_f), (a_b, u_b))):
            for n in range(LRU_CH // LRU_BW):
                sl = slice(n * LRU_BW, (n + 1) * LRU_BW)
                t_r = jnp.tanh(_dot(xcb[:, sl], wrg_ref[d, n]) + brg_ref[d:d + 1, sl])
                t_i = jnp.tanh(_dot(xcb[:, sl], wig_ref[d, n]) + big_ref[d:d + 1, sl])
                half_c = (-0.5 * LRU_C) * _softplus(-lam_ref[d:d + 1, sl])
                log_a = half_c * t_r + half_c
                a = jnp.exp(log_a)
                a_ref[pl.ds(r0, LRU_ROWS), sl] = a
                one_minus_a2 = -jnp.tanh(log_a) * (a * a + 1.0)
                root = jnp.where(one_minus_a2 > 0.0, one_minus_a2 * lax.rsqrt(one_minus_a2), 0.0)
                u_ref[pl.ds(r0, LRU_ROWS), sl] = root * (0.5 * t_i + 0.5) * xc[:, sl]
        return carry

    lax.fori_loop(0, rows // LRU_ROWS, gates, 0)

    block_rows = SCAN_BLOCK * SUBLANES

    def advance(a_ref, u_ref, first_step, order, h, store):
        slab = pl.ds(pl.multiple_of(first_step * SUBLANES, block_rows), block_rows)
        a_all, u_all = a_ref[slab, :], u_ref[slab, :]
        a = [a_all[j * SUBLANES:(j + 1) * SUBLANES] for j in order]
        u = [u_all[j * SUBLANES:(j + 1) * SUBLANES] for j in order]
        a01, u01 = a[1] * a[0], a[1] * u[0] + u[1]
        a23, u23 = a[3] * a[2], a[3] * u[2] + u[3]
        a03, u03 = a23 * a01, a23 * u01 + u23
        h4 = a03 * h + u03
        if store:
            h2 = a01 * h + u01
            states = dict(zip(order, (a[0] * h + u[0], h2, a[2] * h2 + u[2], h4)))
            u_ref[slab, :] = jnp.concatenate([states[j] for j in range(SCAN_BLOCK)], axis=0)
        return h4

    def scan(hf, hb, store):
        ascending, descending = tuple(range(SCAN_BLOCK)), tuple(reversed(range(SCAN_BLOCK)))

        def block(k, carry):
            hf, hb = carry
            hf = advance(a_f, u_f, k * SCAN_BLOCK, ascending, hf, store)
            hb = advance(a_b, u_b, chunk_len - (k + 1) * SCAN_BLOCK, descending, hb, store)
            return hf, hb
        return lax.fori_loop(0, chunk_len // SCAN_BLOCK, block, (hf, hb), unroll=SCAN_UNROLL)

    hf, hb = hf0_ref[...], hb0_ref[...]
    if n_chunks > 1:
        ef, eb = scan(hf, hb, False)
        hf = jnp.where(first_chunk, hf, pltpu.roll(ef, 1, 0))
        hb = jnp.where(last_chunk, hb, pltpu.roll(eb, SUBLANES - 1, 0))
    hf, hb = scan(hf, hb, True)
    if with_state:
        hf_ref[...] = hf
        hb_ref[...] = hb

    def combine(c, carry):
        r = pl.ds(pl.multiple_of(c * LRU_ROWS, LRU_ROWS), LRU_ROWS)
        s_ref[r, :] = ((u_f[r, :] + u_b[r, :]) * gg_ref[r, :]).astype(BF16)
        return carry

    lax.fori_loop(0, rows // LRU_ROWS, combine, 0)


def _rg_lru(grp, l, lx, gg, conv_w, conv_b, w_rg, b_rg, w_ig, b_ig, lam, hf0, hb0, with_state):
    assert grp.n_chunks == 1 or not with_state
    rows = grp.chunk_len * SUBLANES
    nb = LRU_CH // LRU_BW
    tile = pl.BlockSpec((rows, LRU_CH), lambda g, c: (g, c))
    chan = lambda r: pl.BlockSpec((None, r, LRU_CH), lambda g, c: (l, 0, c))
    wblk = pl.BlockSpec((None, 2, nb, LRU_BW, LRU_BW), lambda g, c: (l, 0, c, 0, 0))
    state = pl.BlockSpec((SUBLANES, LRU_CH), lambda g, c: (g, c))
    out_specs = [tile]
    out_shape = [grp.shape_tm(LRU_WIDTH, BF16)]
    if with_state:
        out_specs += [state, state]
        out_shape += [jax.ShapeDtypeStruct((grp.n_pseudo, LRU_WIDTH), F32)] * 2
    scratch = [pltpu.VMEM((rows + (CONV_W - 1) * SUBLANES, LRU_CH), F32)] + [pltpu.VMEM((rows, LRU_CH), F32)] * 4
    return pl.pallas_call(
        functools.partial(_lru_kernel, grp.n_chunks, grp.chunk_len, with_state),
        grid=(grp.lru_groups, LRU_WIDTH // LRU_CH),
        in_specs=[tile, tile, chan(CONV_W), chan(1), wblk, chan(2), wblk, chan(2), chan(2), state, state],
        out_specs=out_specs,
        out_shape=out_shape,
        scratch_shapes=scratch,
        compiler_params=_params(2),
        name="rg_lru",
    )(lx, gg, conv_w, conv_b, w_rg, b_rg, w_ig, b_ig, lam, hf0, hb0)


def _merge_kernel(o_ref, s_ref, gates_ref, x_ref, mod_ref, perm_ref, gpost_ref, wom_ref, wol_ref, wout_ref, y_ref):
    y_mla = _dot(_flat(o_ref[...]), wom_ref[...])
    s = _flat(_seq_major_tile(perm_ref[...], s_ref[...]))
    y_lru = _dot(s, wol_ref[...])
    gates = _flat(gates_ref[...])
    z = gates[:, :D_MODEL] * y_mla + gates[:, D_MODEL:] * y_lru
    y = _dot(z.astype(BF16), wout_ref[...])
    g1 = mod_ref[:, :, 2 * D_MODEL:3 * D_MODEL]
    y_ref[...] = x_ref[...] + g1 * _tiled(_rms(y, gpost_ref[...]))


def _merge(grp, l, o, s, gates, x, mod, perm_t, g_post, w_o_mla, w_o_lru, w_out):
    return pl.pallas_call(
        _merge_kernel,
        grid=grp.grid,
        in_specs=[grp.seq_major(ATTN_OUT), grp.time_major(LRU_WIDTH), grp.seq_major(2 * D_MODEL),
                  grp.seq_major(D_MODEL), grp.per_pseudo(N_MOD * D_MODEL, l), _resident(perm_t.shape),
                  _resident((1, D_MODEL), l), _resident(w_o_mla.shape[1:], l), _resident(w_o_lru.shape[1:], l),
                  _resident(w_out.shape[1:], l)],
        out_specs=grp.seq_major(D_MODEL),
        out_shape=grp.shape3(D_MODEL, F32),
        compiler_params=_params(2),
        name="merge",
    )(o, s, gates, x, mod, perm_t, g_post, w_o_mla, w_o_lru, w_out)


def _ffn_kernel(x_ref, mod_ref, gpre_ref, gpost_ref, w1_ref, w2_ref, y_ref, act_ref):
    x = x_ref[...]
    sh2, sc2, g2 = (mod_ref[:, :, k * D_MODEL:(k + 1) * D_MODEL] for k in (3, 4, 5))
    hb = _flat((_rms(x, gpre_ref[...]) * (1.0 + sc2) + sh2).astype(BF16))
    for c in range(FF_HIDDEN // FFN_CHUNK):
        lo = c * FFN_CHUNK
        g = _dot(hb, w1_ref[:, lo:lo + FFN_CHUNK])
        u = _dot(hb, w1_ref[:, FF_HIDDEN + lo:FF_HIDDEN + lo + FFN_CHUNK])
        act_ref[:, lo:lo + FFN_CHUNK] = (g * _sigmoid(g) * u).astype(BF16)
    y = _dot(act_ref[...], w2_ref[...])
    y_ref[...] = x + g2 * _tiled(_rms(y, gpost_ref[...]))


def _ffn(grp, l, x, mod, g_pre, g_post, w1, w2):
    assert grp.chunk_len % FFN_STEPS == 0
    return pl.pallas_call(
        _ffn_kernel,
        grid=(grp.lru_groups, grp.chunk_len // FFN_STEPS),
        in_specs=[grp.seq_major(D_MODEL, FFN_STEPS), grp.per_pseudo(N_MOD * D_MODEL, l),
                  _resident((1, D_MODEL), l), _resident((1, D_MODEL), l), _resident(w1.shape[1:], l),
                  _resident(w2.shape[1:], l)],
        out_specs=grp.seq_major(D_MODEL, FFN_STEPS),
        out_shape=grp.shape3(D_MODEL, F32),
        scratch_shapes=[pltpu.VMEM((SUBLANES * FFN_STEPS, FF_HIDDEN), BF16)],
        compiler_params=_params(2),
        name="ffn",
    )(x, mod, g_pre, g_post, w1, w2)


def _rot_partner(w):
    half = QK_ROPE // 2
    return jnp.concatenate([-w[..., half:], w[..., :half]], axis=-1)


def _rope_tables(grp):
    n_tokens = grp.seq_len
    rows = n_tokens // GRID_W
    row = jnp.repeat(jnp.arange(rows, dtype=F32), GRID_W)
    col = jnp.tile(jnp.arange(GRID_W, dtype=F32), rows)
    n_freq = QK_ROPE // 4
    inv = ROPE_THETA ** (-jnp.arange(n_freq, dtype=F32) / n_freq)
    ang = jnp.concatenate([row[:, None] * inv, col[:, None] * inv], axis=-1)
    cos, sin = jnp.cos(ang), jnp.sin(ang)
    cos2, sin2 = jnp.concatenate([cos, cos], -1), jnp.concatenate([sin, sin], -1)
    ktab = jnp.concatenate([cos2, sin2, jnp.zeros((n_tokens, LANES - 2 * QK_ROPE), F32)], -1)
    scale = QK_DIM ** -0.5
    tail = jnp.zeros((n_tokens, HEAD_LANES - QK_DIM), F32)
    qcos = jnp.concatenate([jnp.ones((n_tokens, QK_NOPE), F32), cos2, tail], -1) * scale
    qsin = jnp.concatenate([jnp.zeros((n_tokens, QK_NOPE), F32), sin2, tail], -1) * scale

    def by_pseudo(tab):
        tab = tab.reshape(grp.n_chunks, grp.chunk_len, tab.shape[-1])
        return jnp.tile(tab, (SUBLANES // grp.n_chunks, 1, 1))

    return by_pseudo(qcos), by_pseudo(qsin), by_pseudo(ktab)


def kernel(x_prompt, x_sample, cache_ckv, cache_krope, state_lru_fwd, state_lru_bwd, c, c_ctx, w_mod, b_mod, g_pre_mix, g_post_mix, g_pre_ffn, g_post_ffn, w_in, g_q, w_q_up, g_kv, w_kv_up, w_o_mla, conv_w, conv_b, w_rg, b_rg, w_ig, b_ig, lru_lambda, w_o_lru, w_out, w_ffn_in, w_ffn_out):
    batch, seq = x_prompt.shape[:2]
    dec_batch, dec_seq = x_sample.shape[:2]
    past = cache_ckv.shape[2]

    off_kr = Q_LORA + KV_LORA
    w_kr = w_in[:, :, off_kr:off_kr + QK_ROPE]
    w_in_r = jnp.concatenate(
        [w_in[:, :, :off_kr], w_in[:, :, off_kr + QK_ROPE:], w_kr, _rot_partner(w_kr),
         jnp.zeros((DEPTH, D_MODEL, LANES - 2 * QK_ROPE), F32)], axis=-1).astype(BF16)
    wq = w_q_up.reshape(DEPTH, Q_LORA, N_HEADS, QK_DIM)
    q_tail = jnp.zeros((DEPTH, Q_LORA, N_HEADS, HEAD_LANES - QK_DIM), F32)
    q_main = jnp.concatenate([wq, q_tail], axis=-1)
    q_partner = jnp.concatenate([jnp.zeros_like(wq[..., :QK_NOPE]), _rot_partner(wq[..., QK_NOPE:]), q_tail], axis=-1)
    w_q_r = jnp.concatenate([q_main.reshape(DEPTH, Q_LORA, Q_COLS),
                             q_partner.reshape(DEPTH, Q_LORA, Q_COLS)], axis=-1).astype(BF16)
    w_kv_r = w_kv_up.astype(BF16)
    wo = w_o_mla.reshape(DEPTH, N_HEADS, V_DIM, D_MODEL)
    w_o_mla_b = jnp.concatenate([jnp.zeros((DEPTH, N_HEADS, HEAD_LANES - V_DIM, D_MODEL), F32), wo],
                                axis=2).reshape(DEPTH, ATTN_OUT, D_MODEL).astype(BF16)
    w_o_lru_b, w_out_b = w_o_lru.astype(BF16), w_out.astype(BF16)
    w_rg_b, w_ig_b = (0.5 * w_rg).astype(BF16), (0.5 * w_ig).astype(BF16)
    b_rg, b_ig = 0.5 * b_rg, 0.5 * b_ig
    w1_b, w2_b = w_ffn_in.astype(BF16), w_ffn_out.astype(BF16)
    row = lambda v: v.reshape(DEPTH, 1, -1)
    g_pre_mix, g_post_mix, g_pre_ffn, g_post_ffn = map(row, (g_pre_mix, g_post_mix, g_pre_ffn, g_post_ffn))
    g_q, g_kv, conv_b = row(g_q), row(g_kv), row(conv_b)
    perm = _to_time_major()
    perm_b, perm_t_b = jnp.asarray(perm, BF16), jnp.asarray(perm.T, BF16)

    mod_all = _modulation(jnp.concatenate([c_ctx[None, :], c], axis=0), w_mod, b_mod)

    def run_group(grp, x, mod_rows, layer_inputs):
        x = x.reshape(grp.n_pseudo, grp.chunk_len, D_MODEL)
        rope_tabs = _rope_tables(grp) if grp.latent else None
        flat = lambda a: a.reshape(grp.tokens, a.shape[-1])
        mod = mod_all[:, mod_rows][:, :, None, :]
        per_layer = []
        for l in range(DEPTH):
            q, ckv, kv, kr, lx, gg, gates = _in_proj(grp, l, x, mod, perm_b, g_pre_mix, w_in_r, g_q, w_q_r,
                                                     g_kv, w_kv_r, rope_tabs)
            hf0, hb0, kv_ctx, kr_ctx = layer_inputs(l)
            o = _attention(grp, flat(q), flat(kv), flat(kr), kv_ctx, kr_ctx)
            lru = _rg_lru(grp, l, lx, gg, conv_w, conv_b, w_rg_b, b_rg, w_ig_b, b_ig, lru_lambda,
                          hf0, hb0, not grp.latent)
            o = o.reshape(grp.n_pseudo, grp.chunk_len, ATTN_OUT)
            x = _merge(grp, l, o, lru[0], gates, x, mod, perm_t_b, g_post_mix, w_o_mla_b, w_o_lru_b, w_out_b)
            x = _ffn(grp, l, x, mod, g_pre_ffn, g_post_ffn, w1_b, w2_b)
            per_layer.append((ckv, kr) + tuple(lru[1:]))
        return x, per_layer

    ctx = _Group(batch, seq, latent=False)
    zeros_state = jnp.zeros((batch, LRU_WIDTH), F32)
    y_prompt, ctx_layers = run_group(ctx, x_prompt, np.zeros(ctx.n_pseudo, np.int32),
                                     lambda l: (zeros_state, zeros_state, None, None))
    stack = lambda k, shape: jnp.stack([lay[k].reshape(shape) for lay in ctx_layers], axis=1)
    new_ckv = stack(0, (batch, seq, KV_LORA))
    new_krope = stack(1, (batch, seq, QK_ROPE))
    new_lru_fwd = stack(2, (batch, LRU_WIDTH))
    new_lru_bwd = stack(3, (batch, LRU_WIDTH))

    lat = _Group(dec_batch, dec_seq, latent=True)

    def latent_inputs(l):
        per_chunk = lambda s: jnp.repeat(s[:, l], lat.n_chunks, axis=0)
        kv_ctx = _kv_up(l, cache_ckv[:, l].reshape(dec_batch * past, KV_LORA), w_kv_r, past)
        return (per_chunk(state_lru_fwd), per_chunk(state_lru_bwd), kv_ctx,
                cache_krope[:, l].reshape(dec_batch * past, QK_ROPE))

    y_sample, _ = run_group(lat, x_sample, 1 + np.arange(lat.n_pseudo) // lat.n_chunks, latent_inputs)
    return (y_prompt.reshape(batch, seq, D_MODEL), y_sample.reshape(dec_batch, dec_seq, D_MODEL),
            new_ckv, new_krope, new_lru_fwd, new_lru_bwd)
```

```python
import functools

import numpy as np

import jax
import jax.numpy as jnp
from jax import lax
from jax.experimental import pallas as pl
from jax.experimental.pallas import tpu as pltpu

D_MODEL = 1024
DEPTH = 2
GRID_W = 64
N_HEADS = 8
QK_NOPE = 64
QK_ROPE = 32
QK_DIM = QK_NOPE + QK_ROPE
V_DIM = 64
Q_LORA = 384
KV_LORA = 256
ROPE_THETA = 10000.0
LRU_WIDTH = 1024
LRU_BLOCKS = 8
LRU_BW = LRU_WIDTH // LRU_BLOCKS
CONV_W = 4
LRU_C = 8.0
FF_HIDDEN = 2816
N_MOD = 6
EPS = 1e-6

SUBLANES = 8
LANES = 128
VMEM_LIMIT_BYTES = 56 * 1024 * 1024

C_Q = 0
C_KV = C_Q + Q_LORA
C_LX = C_KV + KV_LORA
C_LG = C_LX + LRU_WIDTH
C_MG = C_LG + LRU_WIDTH
C_KR = C_MG + 2 * D_MODEL
IN_COLS_PADDED = C_KR + LANES
HEAD_LANES = LANES
assert QK_DIM <= HEAD_LANES and QK_NOPE + V_DIM == HEAD_LANES
Q_COLS = N_HEADS * HEAD_LANES
KV_COLS = N_HEADS * HEAD_LANES
ATTN_OUT = N_HEADS * HEAD_LANES

STEPS = 64
TILE_ROWS = SUBLANES * STEPS
PERM_STEPS = 32
PERM_ROWS = SUBLANES * PERM_STEPS
assert STEPS % PERM_STEPS == 0
Q_BLOCK = 512
ATTN_SEQS = 4
FFN_STEPS = 64
FFN_CHUNK = 256
LRU_CH = 256
LRU_ROWS = 512
SCAN_BLOCK = 4
SCAN_UNROLL = 2
MOD_TILE = 1536

F32 = jnp.float32
BF16 = jnp.bfloat16


def _params(n_grid_dims):
    return pltpu.CompilerParams(dimension_semantics=("arbitrary",) * n_grid_dims,
                                vmem_limit_bytes=VMEM_LIMIT_BYTES)


def _resident(shape, layer=None):
    zeros = (0,) * len(shape)
    if layer is None:
        return pl.BlockSpec(shape, lambda *_: zeros, pipeline_mode=pl.Buffered(1))
    return pl.BlockSpec((None,) + tuple(shape), lambda *_: (layer,) + zeros, pipeline_mode=pl.Buffered(1))


def _rms(x, g):
    return x * lax.rsqrt(jnp.mean(x * x, axis=-1, keepdims=True) + EPS) * g


def _dot(a, b):
    return jnp.dot(a, b, preferred_element_type=F32)


def _dot_t(a, b):
    return lax.dot_general(a, b, (((1,), (1,)), ((), ())), preferred_element_type=F32)


def _flat(x3):
    return x3.reshape(x3.shape[0] * x3.shape[1], x3.shape[-1])


def _tiled(x2):
    return x2.reshape(SUBLANES, x2.shape[0] // SUBLANES, x2.shape[-1])


def _sigmoid(x):
    return 0.5 * jnp.tanh(0.5 * x) + 0.5


def _to_time_major():
    r = np.arange(PERM_ROWS)
    perm = np.zeros((PERM_ROWS, PERM_ROWS), np.float32)
    perm[r, (r % SUBLANES) * PERM_STEPS + r // SUBLANES] = 1.0
    return perm


def _time_major_rows(perm, x3):
    parts = [_dot(perm, _flat(x3[:, t:t + PERM_STEPS, :])).astype(BF16) for t in range(0, x3.shape[1], PERM_STEPS)]
    return jnp.concatenate(parts, axis=0)


def _seq_major_tile(perm_t, x2):
    parts = [_tiled(_dot(perm_t, x2[r:r + PERM_ROWS, :]).astype(BF16)) for r in range(0, x2.shape[0], PERM_ROWS)]
    return jnp.concatenate(parts, axis=1)


class _Group:
    def __init__(self, batch, seq_len, latent):
        self.batch, self.seq_len, self.latent = batch, seq_len, latent
        self.tokens = batch * seq_len
        self.n_chunks = max(1, SUBLANES // batch)
        self.n_pseudo = batch * self.n_chunks
        assert self.n_chunks in (1, 2) and self.n_pseudo % SUBLANES == 0
        self.chunk_len = seq_len // self.n_chunks
        assert self.chunk_len % STEPS == 0 and seq_len % min(Q_BLOCK, seq_len) == 0
        self.lru_groups = self.n_pseudo // SUBLANES
        self.tiles_per_chunk = self.chunk_len // STEPS
        self.grid = (self.lru_groups, self.tiles_per_chunk)

    def seq_major(self, width, steps=STEPS):
        return pl.BlockSpec((SUBLANES, steps, width), lambda g, i: (g, i, 0))

    def time_major(self, width):
        return pl.BlockSpec((TILE_ROWS, width), lambda g, i: (g * self.tiles_per_chunk + i, 0))

    def per_pseudo(self, width, layer):
        return pl.BlockSpec((None, SUBLANES, 1, width), lambda g, i: (layer, g, 0, 0))

    def by_position(self, width):
        return pl.BlockSpec((SUBLANES, STEPS, width), lambda g, i: (0, i, 0))

    def shape3(self, width, dtype):
        return jax.ShapeDtypeStruct((self.n_pseudo, self.chunk_len, width), dtype)

    def shape_tm(self, width, dtype):
        return jax.ShapeDtypeStruct((self.lru_groups * self.chunk_len * SUBLANES, width), dtype)


def _mod_kernel(c_ref, w_ref, b_ref, o_ref):
    c = c_ref[...]
    a = (c * _sigmoid(c)).astype(BF16)
    o_ref[...] = _dot(a, w_ref[...].astype(BF16)) + b_ref[...]


def _modulation(cond, w_mod, b_mod):
    n = N_MOD * D_MODEL
    rows = cond.shape[0]
    return pl.pallas_call(
        _mod_kernel,
        grid=(DEPTH, n // MOD_TILE),
        in_specs=[pl.BlockSpec((rows, D_MODEL), lambda l, j: (0, 0)),
                  pl.BlockSpec((None, D_MODEL, MOD_TILE), lambda l, j: (l, 0, j)),
                  pl.BlockSpec((None, 1, MOD_TILE), lambda l, j: (l, 0, j))],
        out_specs=pl.BlockSpec((None, rows, MOD_TILE), lambda l, j: (l, 0, j)),
        out_shape=jax.ShapeDtypeStruct((DEPTH, rows, n), F32),
        compiler_params=_params(2),
        name="modulation",
    )(cond, w_mod, b_mod.reshape(DEPTH, 1, n))


def _in_proj_kernel(latent, x_ref, mod_ref, perm_ref, gpre_ref, win_ref, gq_ref, wq_ref, gkv_ref, wkv_ref, *rest):
    if latent:
        cosq_ref, sinq_ref, ktab_ref = rest[:3]
        rest = rest[3:]
    q_ref, ckv_ref, kv_ref, kr_ref, lx_ref, gg_ref, gates_ref = rest
    sh1, sc1 = mod_ref[:, :, 0:D_MODEL], mod_ref[:, :, D_MODEL:2 * D_MODEL]
    hb3 = (_rms(x_ref[...], gpre_ref[...]) * (1.0 + sc1) + sh1).astype(BF16)
    hb = _flat(hb3)

    def proj(lhs, lo, hi):
        return _dot(lhs, win_ref[:, lo:hi])

    qn = _rms(proj(hb, C_Q, C_KV), gq_ref[...]).astype(BF16)
    qf = _dot(qn, wq_ref[...])
    if latent:
        cos, sin = _flat(cosq_ref[...]), _flat(sinq_ref[...])
        for h in range(N_HEADS):
            lanes = slice(h * HEAD_LANES, (h + 1) * HEAD_LANES)
            partner = slice(Q_COLS + h * HEAD_LANES, Q_COLS + (h + 1) * HEAD_LANES)
            q_ref[:, :, lanes] = _tiled((qf[:, lanes] * cos + qf[:, partner] * sin).astype(BF16))
    else:
        q_ref[...] = _tiled((qf * QK_DIM ** -0.5).astype(BF16))

    ckv = _rms(proj(hb, C_KV, C_LX), gkv_ref[...])
    ckv_ref[...] = _tiled(ckv)
    kv_ref[...] = _tiled(_dot(ckv.astype(BF16), wkv_ref[...]).astype(BF16))

    kr = proj(hb, C_KR, IN_COLS_PADDED)
    if latent:
        pr = kr * _flat(ktab_ref[...])
        kr = pr + pltpu.roll(pr, LANES - QK_ROPE, 1)
    kr_ref[...] = _tiled(kr[:, :QK_ROPE])

    gates_ref[...] = _tiled(_sigmoid(proj(hb, C_MG, C_KR)))

    hb_tm = _time_major_rows(perm_ref[...], hb3)
    lx_ref[...] = proj(hb_tm, C_LX, C_LG)
    gg_ref[...] = jax.nn.gelu(proj(hb_tm, C_LG, C_MG))


def _in_proj(grp, l, x, mod, perm, g_pre, w_in, g_q, w_q, g_kv, w_kv, rope_tabs):
    q_cols = w_q.shape[-1] if grp.latent else Q_COLS
    in_specs = [grp.seq_major(D_MODEL), grp.per_pseudo(N_MOD * D_MODEL, l), _resident(perm.shape),
                _resident((1, D_MODEL), l), _resident(w_in.shape[1:], l), _resident((1, Q_LORA), l),
                _resident((Q_LORA, q_cols), l), _resident((1, KV_LORA), l), _resident(w_kv.shape[1:], l)]
    args = [x, mod, perm, g_pre, w_in, g_q, w_q, g_kv, w_kv]
    if grp.latent:
        in_specs += [grp.by_position(HEAD_LANES), grp.by_position(HEAD_LANES), grp.by_position(LANES)]
        args += list(rope_tabs)
    return pl.pallas_call(
        functools.partial(_in_proj_kernel, grp.latent),
        grid=grp.grid,
        in_specs=in_specs,
        out_specs=[grp.seq_major(Q_COLS), grp.seq_major(KV_LORA), grp.seq_major(KV_COLS), grp.seq_major(QK_ROPE),
                   grp.time_major(LRU_WIDTH), grp.time_major(LRU_WIDTH), grp.seq_major(2 * D_MODEL)],
        out_shape=[grp.shape3(Q_COLS, BF16), grp.shape3(KV_LORA, F32), grp.shape3(KV_COLS, BF16),
                   grp.shape3(QK_ROPE, F32), grp.shape_tm(LRU_WIDTH, F32), grp.shape_tm(LRU_WIDTH, F32),
                   grp.shape3(2 * D_MODEL, F32)],
        compiler_params=_params(2),
        name="in_proj",
    )(*args)


def _kv_up_kernel(c_ref, w_ref, o_ref):
    o_ref[...] = _dot(c_ref[...].astype(BF16), w_ref[...]).astype(BF16)


def _kv_up(l, ckv, w_kv, rows_per_step):
    n = ckv.shape[0]
    return pl.pallas_call(
        _kv_up_kernel,
        grid=(n // rows_per_step,),
        in_specs=[pl.BlockSpec((rows_per_step, KV_LORA), lambda i: (i, 0)), _resident(w_kv.shape[1:], l)],
        out_specs=pl.BlockSpec((rows_per_step, KV_COLS), lambda i: (i, 0)),
        out_shape=jax.ShapeDtypeStruct((n, KV_COLS), BF16),
        compiler_params=_params(1),
        name="kv_up_ctx",
    )(ckv, w_kv)


def _attn_kernel(with_ctx, n_seq, q_ref, kv_ref, kr_ref, place_ref, *rest):
    if with_ctx:
        kvc_ref, krc_ref, o_ref = rest
    else:
        (o_ref,) = rest
    is_nope = lax.broadcasted_iota(jnp.int32, (1, HEAD_LANES), 1) < QK_NOPE

    def keys_of(kv, kr, rows):
        kr_placed = _dot(kr[rows, :].astype(BF16), place_ref[...]).astype(BF16)
        return lambda h: jnp.where(is_nope, kv[rows, h * HEAD_LANES:(h + 1) * HEAD_LANES], kr_placed)

    share = lambda ref, j: slice(j * (ref.shape[0] // n_seq), (j + 1) * (ref.shape[0] // n_seq))
    for j in range(n_seq):
        q_rows, k_rows = share(q_ref, j), share(kv_ref, j)
        keys = keys_of(kv_ref, kr_ref, k_rows)
        if with_ctx:
            c_rows = share(kvc_ref, j)
            keys_ctx = keys_of(kvc_ref, krc_ref, c_rows)
        for h in range(N_HEADS):
            lanes = slice(h * HEAD_LANES, (h + 1) * HEAD_LANES)
            q = q_ref[q_rows, lanes]
            s = _dot_t(q, keys(h))
            m = jnp.max(s, axis=-1, keepdims=True)
            if with_ctx:
                sc = _dot_t(q, keys_ctx(h))
                m = jnp.maximum(m, jnp.max(sc, axis=-1, keepdims=True))
            p = jnp.exp(s - m)
            den = jnp.sum(p, axis=-1, keepdims=True)
            o = _dot(p.astype(BF16), kv_ref[k_rows, lanes])
            if with_ctx:
                pc = jnp.exp(sc - m)
                den = den + jnp.sum(pc, axis=-1, keepdims=True)
                o = o + _dot(pc.astype(BF16), kvc_ref[c_rows, lanes])
            o_ref[q_rows, lanes] = (o / den).astype(BF16)


def _attention(grp, q, kv, kr, kv_ctx=None, kr_ctx=None):
    with_ctx = kv_ctx is not None
    q_block = min(Q_BLOCK, grp.seq_len)
    nq = grp.seq_len // q_block
    n_seq = ATTN_SEQS if nq == 1 and grp.batch % ATTN_SEQS == 0 else 1
    seq = lambda rows, width: pl.BlockSpec((n_seq * rows, width), lambda b, i: (b, 0))
    place = np.zeros((QK_ROPE, HEAD_LANES), np.float32)
    place[np.arange(QK_ROPE), QK_NOPE + np.arange(QK_ROPE)] = 1.0
    in_specs = [pl.BlockSpec((n_seq * q_block, Q_COLS), lambda b, i: (b * nq + i, 0)),
                seq(grp.seq_len, KV_COLS), seq(grp.seq_len, QK_ROPE), _resident(place.shape)]
    args = [q, kv, kr, jnp.asarray(place, BF16)]
    if with_ctx:
        past = kv_ctx.shape[0] // grp.batch
        in_specs += [seq(past, KV_COLS), seq(past, QK_ROPE)]
        args += [kv_ctx, kr_ctx]
    return pl.pallas_call(
        functools.partial(_attn_kernel, with_ctx, n_seq),
        grid=(grp.batch // n_seq, nq),
        in_specs=in_specs,
        out_specs=pl.BlockSpec((n_seq * q_block, ATTN_OUT), lambda b, i: (b * nq + i, 0)),
        out_shape=jax.ShapeDtypeStruct((grp.tokens, ATTN_OUT), BF16),
        compiler_params=_params(2),
        name="attention",
    )(*args)


def _softplus(z):
    return jnp.maximum(z, 0.0) + jnp.log1p(jnp.exp(-jnp.abs(z)))


def _lru_kernel(n_chunks, chunk_len, with_state, x_ref, gg_ref, cw_ref, cb_ref, wrg_ref, brg_ref,
                wig_ref, big_ref, lam_ref, hf0_ref, hb0_ref, *rest):
    if with_state:
        s_ref, hf_ref, hb_ref, xpad, a_f, u_f, a_b, u_b = rest
    else:
        s_ref, xpad, a_f, u_f, a_b, u_b = rest
    rows = chunk_len * SUBLANES
    halo = (CONV_W // 2) * SUBLANES
    sub = lax.broadcasted_iota(jnp.int32, (SUBLANES, LRU_CH), 0)
    first_chunk = (sub % n_chunks) == 0
    last_chunk = (sub % n_chunks) == n_chunks - 1

    def from_prev(v):
        return jnp.where(first_chunk, 0.0, pltpu.roll(v, 1, 0))

    def from_next(v):
        return jnp.where(last_chunk, 0.0, pltpu.roll(v, SUBLANES - 1, 0))

    xpad[halo:halo + rows, :] = x_ref[...]
    xpad[0:SUBLANES, :] = from_prev(x_ref[rows - 2 * SUBLANES:rows - SUBLANES, :])
    xpad[SUBLANES:halo, :] = from_prev(x_ref[rows - SUBLANES:rows, :])
    xpad[halo + rows:halo + rows + SUBLANES, :] = from_next(x_ref[0:SUBLANES, :])

    def gates(c, carry):
        r0 = pl.multiple_of(c * LRU_ROWS, LRU_ROWS)
        xc = cb_ref[...]
        for k in range(CONV_W):
            xc = xc + xpad[pl.ds(r0 + k * SUBLANES, LRU_ROWS), :] * cw_ref[k:k + 1, :]
        xcb = xc.astype(BF16)
        for d, (a_ref, u_ref) in enumerate(((a_f, u_f), (a_b, u_b))):
            for n in range(LRU_CH // LRU_BW):
                sl = slice(n * LRU_BW, (n + 1) * LRU_BW)
                t_r = jnp.tanh(_dot(xcb[:, sl], wrg_ref[d, n]) + brg_ref[d:d + 1, sl])
                t_i = jnp.tanh(_dot(xcb[:, sl], wig_ref[d, n]) + big_ref[d:d + 1, sl])
                half_c = (-0.5 * LRU_C) * _softplus(-lam_ref[d:d + 1, sl])
                log_a = half_c * t_r + half_c
                a = jnp.exp(log_a)
                a_ref[pl.ds(r0, LRU_ROWS), sl] = a
                one_minus_a2 = -jnp.tanh(log_a) * (a * a + 1.0)
                root = jnp.where(one_minus_a2 > 0.0, one_minus_a2 * lax.rsqrt(one_minus_a2), 0.0)
                u_ref[pl.ds(r0, LRU_ROWS), sl] = root * (0.5 * t_i + 0.5) * xc[:, sl]
        return carry

    lax.fori_loop(0, rows // LRU_ROWS, gates, 0)

    block_rows = SCAN_BLOCK * SUBLANES

    def advance(a_ref, u_ref, first_step, order, h, store):
        slab = pl.ds(pl.multiple_of(first_step * SUBLANES, block_rows), block_rows)
        a_all, u_all = a_ref[slab, :], u_ref[slab, :]
        a = [a_all[j * SUBLANES:(j + 1) * SUBLANES] for j in order]
        u = [u_all[j * SUBLANES:(j + 1) * SUBLANES] for j in order]
        a01, u01 = a[1] * a[0], a[1] * u[0] + u[1]
        a23, u23 = a[3] * a[2], a[3] * u[2] + u[3]
        a03, u03 = a23 * a01, a23 * u01 + u23
        h4 = a03 * h + u03
        if store:
            h2 = a01 * h + u01
            states = dict(zip(order, (a[0] * h + u[0], h2, a[2] * h2 + u[2], h4)))
            u_ref[slab, :] = jnp.concatenate([states[j] for j in range(SCAN_BLOCK)], axis=0)
        return h4

    def scan(hf, hb, store):
        ascending, descending = tuple(range(SCAN_BLOCK)), tuple(reversed(range(SCAN_BLOCK)))

        def block(k, carry):
            hf, hb = carry
            hf = advance(a_f, u_f, k * SCAN_BLOCK, ascending, hf, store)
            hb = advance(a_b, u_b, chunk_len - (k + 1) * SCAN_BLOCK, descending, hb, store)
            return hf, hb
        return lax.fori_loop(0, chunk_len // SCAN_BLOCK, block, (hf, hb), unroll=SCAN_UNROLL)

    hf, hb = hf0_ref[...], hb0_ref[...]
    if n_chunks > 1:
        ef, eb = scan(hf, hb, False)
        hf = jnp.where(first_chunk, hf, pltpu.roll(ef, 1, 0))
        hb = jnp.where(last_chunk, hb, pltpu.roll(eb, SUBLANES - 1, 0))
    hf, hb = scan(hf, hb, True)
    if with_state:
        hf_ref[...] = hf
        hb_ref[...] = hb

    def combine(c, carry):
        r = pl.ds(pl.multiple_of(c * LRU_ROWS, LRU_ROWS), LRU_ROWS)
        s_ref[r, :] = ((u_f[r, :] + u_b[r, :]) * gg_ref[r, :]).astype(BF16)
        return carry

    lax.fori_loop(0, rows // LRU_ROWS, combine, 0)


def _rg_lru(grp, l, lx, gg, conv_w, conv_b, w_rg, b_rg, w_ig, b_ig, lam, hf0, hb0, with_state):
    assert grp.n_chunks == 1 or not with_state
    rows = grp.chunk_len * SUBLANES
    nb = LRU_CH // LRU_BW
    tile = pl.BlockSpec((rows, LRU_CH), lambda g, c: (g, c))
    chan = lambda r: pl.BlockSpec((None, r, LRU_CH), lambda g, c: (l, 0, c))
    wblk = pl.BlockSpec((None, 2, nb, LRU_BW, LRU_BW), lambda g, c: (l, 0, c, 0, 0))
    state = pl.BlockSpec((SUBLANES, LRU_CH), lambda g, c: (g, c))
    out_specs = [tile]
    out_shape = [grp.shape_tm(LRU_WIDTH, BF16)]
    if with_state:
        out_specs += [state, state]
        out_shape += [jax.ShapeDtypeStruct((grp.n_pseudo, LRU_WIDTH), F32)] * 2
    scratch = [pltpu.VMEM((rows + (CONV_W - 1) * SUBLANES, LRU_CH), F32)] + [pltpu.VMEM((rows, LRU_CH), F32)] * 4
    return pl.pallas_call(
        functools.partial(_lru_kernel, grp.n_chunks, grp.chunk_len, with_state),
        grid=(grp.lru_groups, LRU_WIDTH // LRU_CH),
        in_specs=[tile, tile, chan(CONV_W), chan(1), wblk, chan(2), wblk, chan(2), chan(2), state, state],
        out_specs=out_specs,
        out_shape=out_shape,
        scratch_shapes=scratch,
        compiler_params=_params(2),
        name="rg_lru",
    )(lx, gg, conv_w, conv_b, w_rg, b_rg, w_ig, b_ig, lam, hf0, hb0)


def _merge_kernel(o_ref, s_ref, gates_ref, x_ref, mod_ref, perm_ref, gpost_ref, wom_ref, wol_ref, wout_ref, y_ref):
    y_mla = _dot(_flat(o_ref[...]), wom_ref[...])
    s = _flat(_seq_major_tile(perm_ref[...], s_ref[...]))
    y_lru = _dot(s, wol_ref[...])
    gates = _flat(gates_ref[...])
    z = gates[:, :D_MODEL] * y_mla + gates[:, D_MODEL:] * y_lru
    y = _dot(z.astype(BF16), wout_ref[...])
    g1 = mod_ref[:, :, 2 * D_MODEL:3 * D_MODEL]
    y_ref[...] = x_ref[...] + g1 * _tiled(_rms(y, gpost_ref[...]))


def _merge(grp, l, o, s, gates, x, mod, perm_t, g_post, w_o_mla, w_o_lru, w_out):
    return pl.pallas_call(
        _merge_kernel,
        grid=grp.grid,
        in_specs=[grp.seq_major(ATTN_OUT), grp.time_major(LRU_WIDTH), grp.seq_major(2 * D_MODEL),
                  grp.seq_major(D_MODEL), grp.per_pseudo(N_MOD * D_MODEL, l), _resident(perm_t.shape),
                  _resident((1, D_MODEL), l), _resident(w_o_mla.shape[1:], l), _resident(w_o_lru.shape[1:], l),
                  _resident(w_out.shape[1:], l)],
        out_specs=grp.seq_major(D_MODEL),
        out_shape=grp.shape3(D_MODEL, F32),
        compiler_params=_params(2),
        name="merge",
    )(o, s, gates, x, mod, perm_t, g_post, w_o_mla, w_o_lru, w_out)


def _ffn_kernel(x_ref, mod_ref, gpre_ref, gpost_ref, w1_ref, w2_ref, y_ref, act_ref):
    x = x_ref[...]
    sh2, sc2, g2 = (mod_ref[:, :, k * D_MODEL:(k + 1) * D_MODEL] for k in (3, 4, 5))
    hb = _flat((_rms(x, gpre_ref[...]) * (1.0 + sc2) + sh2).astype(BF16))
    for c in range(FF_HIDDEN // FFN_CHUNK):
        lo = c * FFN_CHUNK
        g = _dot(hb, w1_ref[:, lo:lo + FFN_CHUNK])
        u = _dot(hb, w1_ref[:, FF_HIDDEN + lo:FF_HIDDEN + lo + FFN_CHUNK])
        act_ref[:, lo:lo + FFN_CHUNK] = (g * _sigmoid(g) * u).astype(BF16)
    y = _dot(act_ref[...], w2_ref[...])
    y_ref[...] = x + g2 * _tiled(_rms(y, gpost_ref[...]))


def _ffn(grp, l, x, mod, g_pre, g_post, w1, w2):
    assert grp.chunk_len % FFN_STEPS == 0
    return pl.pallas_call(
        _ffn_kernel,
        grid=(grp.lru_groups, grp.chunk_len // FFN_STEPS),
        in_specs=[grp.seq_major(D_MODEL, FFN_STEPS), grp.per_pseudo(N_MOD * D_MODEL, l),
                  _resident((1, D_MODEL), l), _resident((1, D_MODEL), l), _resident(w1.shape[1:], l),
                  _resident(w2.shape[1:], l)],
        out_specs=grp.seq_major(D_MODEL, FFN_STEPS),
        out_shape=grp.shape3(D_MODEL, F32),
        scratch_shapes=[pltpu.VMEM((SUBLANES * FFN_STEPS, FF_HIDDEN), BF16)],
        compiler_params=_params(2),
        name="ffn",
    )(x, mod, g_pre, g_post, w1, w2)


def _rot_partner(w):
    half = QK_ROPE // 2
    return jnp.concatenate([-w[..., half:], w[..., :half]], axis=-1)


def _rope_tables(grp):
    n_tokens = grp.seq_len
    rows = n_tokens // GRID_W
    row = jnp.repeat(jnp.arange(rows, dtype=F32), GRID_W)
    col = jnp.tile(jnp.arange(GRID_W, dtype=F32), rows)
    n_freq = QK_ROPE // 4
    inv = ROPE_THETA ** (-jnp.arange(n_freq, dtype=F32) / n_freq)
    ang = jnp.concatenate([row[:, None] * inv, col[:, None] * inv], axis=-1)
    cos, sin = jnp.cos(ang), jnp.sin(ang)
    cos2, sin2 = jnp.concatenate([cos, cos], -1), jnp.concatenate([sin, sin], -1)
    ktab = jnp.concatenate([cos2, sin2, jnp.zeros((n_tokens, LANES - 2 * QK_ROPE), F32)], -1)
    scale = QK_DIM ** -0.5
    tail = jnp.zeros((n_tokens, HEAD_LANES - QK_DIM), F32)
    qcos = jnp.concatenate([jnp.ones((n_tokens, QK_NOPE), F32), cos2, tail], -1) * scale
    qsin = jnp.concatenate([jnp.zeros((n_tokens, QK_NOPE), F32), sin2, tail], -1) * scale

    def by_pseudo(tab):
        tab = tab.reshape(grp.n_chunks, grp.chunk_len, tab.shape[-1])
        return jnp.tile(tab, (SUBLANES // grp.n_chunks, 1, 1))

    return by_pseudo(qcos), by_pseudo(qsin), by_pseudo(ktab)


def kernel(x_prompt, x_sample, cache_ckv, cache_krope, state_lru_fwd, state_lru_bwd, c, c_ctx, w_mod, b_mod, g_pre_mix, g_post_mix, g_pre_ffn, g_post_ffn, w_in, g_q, w_q_up, g_kv, w_kv_up, w_o_mla, conv_w, conv_b, w_rg, b_rg, w_ig, b_ig, lru_lambda, w_o_lru, w_out, w_ffn_in, w_ffn_out):
    batch, seq = x_prompt.shape[:2]
    dec_batch, dec_seq = x_sample.shape[:2]
    past = cache_ckv.shape[2]

    off_kr = Q_LORA + KV_LORA
    w_kr = w_in[:, :, off_kr:off_kr + QK_ROPE]
    w_in_r = jnp.concatenate(
        [w_in[:, :, :off_kr], w_in[:, :, off_kr + QK_ROPE:], w_kr, _rot_partner(w_kr),
         jnp.zeros((DEPTH, D_MODEL, LANES - 2 * QK_ROPE), F32)], axis=-1).astype(BF16)
    wq = w_q_up.reshape(DEPTH, Q_LORA, N_HEADS, QK_DIM)
    q_tail = jnp.zeros((DEPTH, Q_LORA, N_HEADS, HEAD_LANES - QK_DIM), F32)
    q_main = jnp.concatenate([wq, q_tail], axis=-1)
    q_partner = jnp.concatenate([jnp.zeros_like(wq[..., :QK_NOPE]), _rot_partner(wq[..., QK_NOPE:]), q_tail], axis=-1)
    w_q_r = jnp.concatenate([q_main.reshape(DEPTH, Q_LORA, Q_COLS),
                             q_partner.reshape(DEPTH, Q_LORA, Q_COLS)], axis=-1).astype(BF16)
    w_kv_r = w_kv_up.astype(BF16)
    wo = w_o_mla.reshape(DEPTH, N_HEADS, V_DIM, D_MODEL)
    w_o_mla_b = jnp.concatenate([jnp.zeros((DEPTH, N_HEADS, HEAD_LANES - V_DIM, D_MODEL), F32), wo],
                                axis=2).reshape(DEPTH, ATTN_OUT, D_MODEL).astype(BF16)
    w_o_lru_b, w_out_b = w_o_lru.astype(BF16), w_out.astype(BF16)
    w_rg_b, w_ig_b = (0.5 * w_rg).astype(BF16), (0.5 * w_ig).astype(BF16)
    b_rg, b_ig = 0.5 * b_rg, 0.5 * b_ig
    w1_b, w2_b = w_ffn_in.astype(BF16), w_ffn_out.astype(BF16)
    row = lambda v: v.reshape(DEPTH, 1, -1)
    g_pre_mix, g_post_mix, g_pre_ffn, g_post_ffn = map(row, (g_pre_mix, g_post_mix, g_pre_ffn, g_post_ffn))
    g_q, g_kv, conv_b = row(g_q), row(g_kv), row(conv_b)
    perm = _to_time_major()
    perm_b, perm_t_b = jnp.asarray(perm, BF16), jnp.asarray(perm.T, BF16)

    mod_all = _modulation(jnp.concatenate([c_ctx[None, :], c], axis=0), w_mod, b_mod)

    def run_group(grp, x, mod_rows, layer_inputs):
        x = x.reshape(grp.n_pseudo, grp.chunk_len, D_MODEL)
        rope_tabs = _rope_tables(grp) if grp.latent else None
        flat = lambda a: a.reshape(grp.tokens, a.shape[-1])
        mod = mod_all[:, mod_rows][:, :, None, :]
        per_layer = []
        for l in range(DEPTH):
            q, ckv, kv, kr, lx, gg, gates = _in_proj(grp, l, x, mod, perm_b, g_pre_mix, w_in_r, g_q, w_q_r,
                                                     g_kv, w_kv_r, rope_tabs)
            hf0, hb0, kv_ctx, kr_ctx = layer_inputs(l)
            o = _attention(grp, flat(q), flat(kv), flat(kr), kv_ctx, kr_ctx)
            lru = _rg_lru(grp, l, lx, gg, conv_w, conv_b, w_rg_b, b_rg, w_ig_b, b_ig, lru_lambda,
                          hf0, hb0, not grp.latent)
            o = o.reshape(grp.n_pseudo, grp.chunk_len, ATTN_OUT)
            x = _merge(grp, l, o, lru[0], gates, x, mod, perm_t_b, g_post_mix, w_o_mla_b, w_o_lru_b, w_out_b)
            x = _ffn(grp, l, x, mod, g_pre_ffn, g_post_ffn, w1_b, w2_b)
            per_layer.append((ckv, kr) + tuple(lru[1:]))
        return x, per_layer

    ctx = _Group(batch, seq, latent=False)
    zeros_state = jnp.zeros((batch, LRU_WIDTH), F32)
    y_prompt, ctx_layers = run_group(ctx, x_prompt, np.zeros(ctx.n_pseudo, np.int32),
                                     lambda l: (zeros_state, zeros_state, None, None))
    stack = lambda k, shape: jnp.stack([lay[k].reshape(shape) for lay in ctx_layers], axis=1)
    new_ckv = stack(0, (batch, seq, KV_LORA))
    new_krope = stack(1, (batch, seq, QK_ROPE))
    new_lru_fwd = stack(2, (batch, LRU_WIDTH))
    new_lru_bwd = stack(3, (batch, LRU_WIDTH))

    lat = _Group(dec_batch, dec_seq, latent=True)

    def latent_inputs(l):
        per_chunk = lambda s: jnp.repeat(s[:, l], lat.n_chunks, axis=0)
        kv_ctx = _kv_up(l, cache_ckv[:, l].reshape(dec_batch * past, KV_LORA), w_kv_r, past)
        return (per_chunk(state_lru_fwd), per_chunk(state_lru_bwd), kv_ctx,
                cache_krope[:, l].reshape(dec_batch * past, QK_ROPE))

    y_sample, _ = run_group(lat, x_sample, 1 + np.arange(lat.n_pseudo) // lat.n_chunks, latent_inputs)
    return (y_prompt.reshape(batch, seq, D_MODEL), y_sample.reshape(dec_batch, dec_seq, D_MODEL),
            new_ckv, new_krope, new_lru_fwd, new_lru_bwd)
```

```python
import functools

import numpy as np

import jax
import jax.numpy as jnp
from jax import lax
from jax.experimental import pallas as pl
from jax.experimental.pallas import tpu as pltpu

D_MODEL = 1024
DEPTH = 2
GRID_W = 64
N_HEADS = 8
QK_NOPE = 64
QK_ROPE = 32
QK_DIM = QK_NOPE + QK_ROPE
V_DIM = 64
Q_LORA = 384
KV_LORA = 256
ROPE_THETA = 10000.0
LRU_WIDTH = 1024
LRU_BLOCKS = 8
LRU_BW = LRU_WIDTH // LRU_BLOCKS
CONV_W = 4
LRU_C = 8.0
FF_HIDDEN = 2816
N_MOD = 6
EPS = 1e-6

SUBLANES = 8
LANES = 128
VMEM_LIMIT_BYTES = 56 * 1024 * 1024

C_Q = 0
C_KV = C_Q + Q_LORA
C_LX = C_KV + KV_LORA
C_LG = C_LX + LRU_WIDTH
C_MG = C_LG + LRU_WIDTH
C_KR = C_MG + 2 * D_MODEL
IN_COLS_PADDED = C_KR + LANES
HEAD_LANES = LANES
assert QK_DIM <= HEAD_LANES and QK_NOPE + V_DIM == HEAD_LANES
Q_COLS = N_HEADS * HEAD_LANES
KV_COLS = N_HEADS * HEAD_LANES
ATTN_OUT = N_HEADS * HEAD_LANES

STEPS = 64
TILE_ROWS = SUBLANES * STEPS
PERM_STEPS = 32
PERM_ROWS = SUBLANES * PERM_STEPS
assert STEPS % PERM_STEPS == 0
Q_BLOCK = 512
ATTN_SEQS = 4
FFN_STEPS = 64
FFN_CHUNK = 256
LRU_CH = 256
LRU_ROWS = 512
SCAN_BLOCK = 4
SCAN_UNROLL = 2
MOD_TILE = 1536

F32 = jnp.float32
BF16 = jnp.bfloat16


def _params(n_grid_dims):
    return pltpu.CompilerParams(dimension_semantics=("arbitrary",) * n_grid_dims,
                                vmem_limit_bytes=VMEM_LIMIT_BYTES)


def _resident(shape, layer=None):
    zeros = (0,) * len(shape)
    if layer is None:
        return pl.BlockSpec(shape, lambda *_: zeros, pipeline_mode=pl.Buffered(1))
    return pl.BlockSpec((None,) + tuple(shape), lambda *_: (layer,) + zeros, pipeline_mode=pl.Buffered(1))


def _rms(x, g):
    return x * lax.rsqrt(jnp.mean(x * x, axis=-1, keepdims=True) + EPS) * g


def _dot(a, b):
    return jnp.dot(a, b, preferred_element_type=F32)


def _dot_t(a, b):
    return lax.dot_general(a, b, (((1,), (1,)), ((), ())), preferred_element_type=F32)


def _flat(x3):
    return x3.reshape(x3.shape[0] * x3.shape[1], x3.shape[-1])


def _tiled(x2):
    return x2.reshape(SUBLANES, x2.shape[0] // SUBLANES, x2.shape[-1])


def _sigmoid(x):
    return 0.5 * jnp.tanh(0.5 * x) + 0.5


def _to_time_major():
    r = np.arange(PERM_ROWS)
    perm = np.zeros((PERM_ROWS, PERM_ROWS), np.float32)
    perm[r, (r % SUBLANES) * PERM_STEPS + r // SUBLANES] = 1.0
    return perm


def _time_major_rows(perm, x3):
    parts = [_dot(perm, _flat(x3[:, t:t + PERM_STEPS, :])).astype(BF16) for t in range(0, x3.shape[1], PERM_STEPS)]
    return jnp.concatenate(parts, axis=0)


def _seq_major_tile(perm_t, x2):
    parts = [_tiled(_dot(perm_t, x2[r:r + PERM_ROWS, :]).astype(BF16)) for r in range(0, x2.shape[0], PERM_ROWS)]
    return jnp.concatenate(parts, axis=1)


class _Group:
    def __init__(self, batch, seq_len, latent):
        self.batch, self.seq_len, self.latent = batch, seq_len, latent
        self.tokens = batch * seq_len
        self.n_chunks = max(1, SUBLANES // batch)
        self.n_pseudo = batch * self.n_chunks
        assert self.n_chunks in (1, 2) and self.n_pseudo % SUBLANES == 0
        self.chunk_len = seq_len // self.n_chunks
        assert self.chunk_len % STEPS == 0 and seq_len % min(Q_BLOCK, seq_len) == 0
        self.lru_groups = self.n_pseudo // SUBLANES
        self.tiles_per_chunk = self.chunk_len // STEPS
        self.grid = (self.lru_groups, self.tiles_per_chunk)

    def seq_major(self, width, steps=STEPS):
        return pl.BlockSpec((SUBLANES, steps, width), lambda g, i: (g, i, 0))

    def time_major(self, width):
        return pl.BlockSpec((TILE_ROWS, width), lambda g, i: (g * self.tiles_per_chunk + i, 0))

    def per_pseudo(self, width, layer):
        return pl.BlockSpec((None, SUBLANES, 1, width), lambda g, i: (layer, g, 0, 0))

    def by_position(self, width):
        return pl.BlockSpec((SUBLANES, STEPS, width), lambda g, i: (0, i, 0))

    def shape3(self, width, dtype):
        return jax.ShapeDtypeStruct((self.n_pseudo, self.chunk_len, width), dtype)

    def shape_tm(self, width, dtype):
        return jax.ShapeDtypeStruct((self.lru_groups * self.chunk_len * SUBLANES, width), dtype)


def _mod_kernel(c_ref, w_ref, b_ref, o_ref):
    c = c_ref[...]
    a = (c * _sigmoid(c)).astype(BF16)
    o_ref[...] = _dot(a, w_ref[...].astype(BF16)) + b_ref[...]


def _modulation(cond, w_mod, b_mod):
    n = N_MOD * D_MODEL
    rows = cond.shape[0]
    return pl.pallas_call(
        _mod_kernel,
        grid=(DEPTH, n // MOD_TILE),
        in_specs=[pl.BlockSpec((rows, D_MODEL), lambda l, j: (0, 0)),
                  pl.BlockSpec((None, D_MODEL, MOD_TILE), lambda l, j: (l, 0, j)),
                  pl.BlockSpec((None, 1, MOD_TILE), lambda l, j: (l, 0, j))],
        out_specs=pl.BlockSpec((None, rows, MOD_TILE), lambda l, j: (l, 0, j)),
        out_shape=jax.ShapeDtypeStruct((DEPTH, rows, n), F32),
        compiler_params=_params(2),
        name="modulation",
    )(cond, w_mod, b_mod.reshape(DEPTH, 1, n))


def _w_in_layout_kernel(w_ref, o_ref):
    off_kr = Q_LORA + KV_LORA
    lane = lax.broadcasted_iota(jnp.int32, (1, LANES), 1)
    half = QK_ROPE // 2
    o_ref[:, C_Q:C_LX] = w_ref[:, :off_kr].astype(BF16)
    o_ref[:, C_LX:C_KR] = w_ref[:, off_kr + QK_ROPE:].astype(BF16)
    tile = w_ref[:, off_kr:off_kr + LANES]
    partner = jnp.where(lane < half, -pltpu.roll(tile, LANES - half, 1), pltpu.roll(tile, half, 1))
    seg = jnp.where(lane < QK_ROPE, tile, jnp.where(lane < 2 * QK_ROPE, pltpu.roll(partner, QK_ROPE, 1), 0.0))
    o_ref[:, C_KR:IN_COLS_PADDED] = seg.astype(BF16)


def _w_in_layout(w_in):
    rows = 256
    in_cols = w_in.shape[-1]
    return pl.pallas_call(
        _w_in_layout_kernel,
        grid=(DEPTH, D_MODEL // rows),
        in_specs=[pl.BlockSpec((None, rows, in_cols), lambda l, i: (l, i, 0))],
        out_specs=pl.BlockSpec((None, rows, IN_COLS_PADDED), lambda l, i: (l, i, 0)),
        out_shape=jax.ShapeDtypeStruct((DEPTH, D_MODEL, IN_COLS_PADDED), BF16),
        compiler_params=_params(2),
        name="w_in_layout",
    )(w_in)


def _in_proj_kernel(latent, x_ref, mod_ref, perm_ref, gpre_ref, win_ref, gq_ref, wq_ref, gkv_ref, wkv_ref, *rest):
    if latent:
        cosq_ref, sinq_ref, ktab_ref = rest[:3]
        rest = rest[3:]
    q_ref, ckv_ref, kv_ref, kr_ref, lx_ref, gg_ref, gates_ref = rest
    sh1, sc1 = mod_ref[:, :, 0:D_MODEL], mod_ref[:, :, D_MODEL:2 * D_MODEL]
    hb3 = (_rms(x_ref[...], gpre_ref[...]) * (1.0 + sc1) + sh1).astype(BF16)
    hb = _flat(hb3)

    def proj(lhs, lo, hi):
        return _dot(lhs, win_ref[:, lo:hi])

    qn = _rms(proj(hb, C_Q, C_KV), gq_ref[...]).astype(BF16)
    qf = _dot(qn, wq_ref[...])
    if latent:
        cos, sin = _flat(cosq_ref[...]), _flat(sinq_ref[...])
        for h in range(N_HEADS):
            lanes = slice(h * HEAD_LANES, (h + 1) * HEAD_LANES)
            partner = slice(Q_COLS + h * HEAD_LANES, Q_COLS + (h + 1) * HEAD_LANES)
            q_ref[:, :, lanes] = _tiled((qf[:, lanes] * cos + qf[:, partner] * sin).astype(BF16))
    else:
        q_ref[...] = _tiled((qf * QK_DIM ** -0.5).astype(BF16))

    ckv = _rms(proj(hb, C_KV, C_LX), gkv_ref[...])
    ckv_ref[...] = _tiled(ckv)
    kv_ref[...] = _tiled(_dot(ckv.astype(BF16), wkv_ref[...]).astype(BF16))

    kr = proj(hb, C_KR, IN_COLS_PADDED)
    if latent:
        pr = kr * _flat(ktab_ref[...])
        kr = pr + pltpu.roll(pr, LANES - QK_ROPE, 1)
    kr_ref[...] = _tiled(kr[:, :QK_ROPE])

    gates_ref[...] = _tiled(_sigmoid(proj(hb, C_MG, C_KR)))

    hb_tm = _time_major_rows(perm_ref[...], hb3)
    lx_ref[...] = proj(hb_tm, C_LX, C_LG)
    gg_ref[...] = jax.nn.gelu(proj(hb_tm, C_LG, C_MG))


def _in_proj(grp, l, x, mod, perm, g_pre, w_in, g_q, w_q, g_kv, w_kv, rope_tabs):
    q_cols = w_q.shape[-1] if grp.latent else Q_COLS
    in_specs = [grp.seq_major(D_MODEL), grp.per_pseudo(N_MOD * D_MODEL, l), _resident(perm.shape),
                _resident((1, D_MODEL), l), _resident(w_in.shape[1:], l), _resident((1, Q_LORA), l),
                _resident((Q_LORA, q_cols), l), _resident((1, KV_LORA), l), _resident(w_kv.shape[1:], l)]
    args = [x, mod, perm, g_pre, w_in, g_q, w_q, g_kv, w_kv]
    if grp.latent:
        in_specs += [grp.by_position(HEAD_LANES), grp.by_position(HEAD_LANES), grp.by_position(LANES)]
        args += list(rope_tabs)
    return pl.pallas_call(
        functools.partial(_in_proj_kernel, grp.latent),
        grid=grp.grid,
        in_specs=in_specs,
        out_specs=[grp.seq_major(Q_COLS), grp.seq_major(KV_LORA), grp.seq_major(KV_COLS), grp.seq_major(QK_ROPE),
                   grp.time_major(LRU_WIDTH), grp.time_major(LRU_WIDTH), grp.seq_major(2 * D_MODEL)],
        out_shape=[grp.shape3(Q_COLS, BF16), grp.shape3(KV_LORA, F32), grp.shape3(KV_COLS, BF16),
                   grp.shape3(QK_ROPE, F32), grp.shape_tm(LRU_WIDTH, F32), grp.shape_tm(LRU_WIDTH, F32),
                   grp.shape3(2 * D_MODEL, F32)],
        compiler_params=_params(2),
        name="in_proj",
    )(*args)


def _kv_up_kernel(c_ref, w_ref, o_ref):
    o_ref[...] = _dot(c_ref[...].astype(BF16), w_ref[...]).astype(BF16)


def _kv_up(l, ckv, w_kv, rows_per_step):
    n = ckv.shape[0]
    return pl.pallas_call(
        _kv_up_kernel,
        grid=(n // rows_per_step,),
        in_specs=[pl.BlockSpec((rows_per_step, KV_LORA), lambda i: (i, 0)), _resident(w_kv.shape[1:], l)],
        out_specs=pl.BlockSpec((rows_per_step, KV_COLS), lambda i: (i, 0)),
        out_shape=jax.ShapeDtypeStruct((n, KV_COLS), BF16),
        compiler_params=_params(1),
        name="kv_up_ctx",
    )(ckv, w_kv)


def _attn_kernel(with_ctx, n_seq, q_ref, kv_ref, kr_ref, place_ref, *rest):
    if with_ctx:
        kvc_ref, krc_ref, o_ref = rest
    else:
        (o_ref,) = rest
    is_nope = lax.broadcasted_iota(jnp.int32, (1, HEAD_LANES), 1) < QK_NOPE

    def keys_of(kv, kr, rows):
        kr_placed = _dot(kr[rows, :].astype(BF16), place_ref[...]).astype(BF16)
        return lambda h: jnp.where(is_nope, kv[rows, h * HEAD_LANES:(h + 1) * HEAD_LANES], kr_placed)

    share = lambda ref, j: slice(j * (ref.shape[0] // n_seq), (j + 1) * (ref.shape[0] // n_seq))
    for j in range(n_seq):
        q_rows, k_rows = share(q_ref, j), share(kv_ref, j)
        keys = keys_of(kv_ref, kr_ref, k_rows)
        if with_ctx:
            c_rows = share(kvc_ref, j)
            keys_ctx = keys_of(kvc_ref, krc_ref, c_rows)
        for h in range(N_HEADS):
            lanes = slice(h * HEAD_LANES, (h + 1) * HEAD_LANES)
            q = q_ref[q_rows, lanes]
            s = _dot_t(q, keys(h))
            m = jnp.max(s, axis=-1, keepdims=True)
            if with_ctx:
                sc = _dot_t(q, keys_ctx(h))
                m = jnp.maximum(m, jnp.max(sc, axis=-1, keepdims=True))
            p = jnp.exp(s - m)
            den = jnp.sum(p, axis=-1, keepdims=True)
            o = _dot(p.astype(BF16), kv_ref[k_rows, lanes])
            if with_ctx:
                pc = jnp.exp(sc - m)
                den = den + jnp.sum(pc, axis=-1, keepdims=True)
                o = o + _dot(pc.astype(BF16), kvc_ref[c_rows, lanes])
            o_ref[q_rows, lanes] = (o / den).astype(BF16)


def _attention(grp, q, kv, kr, kv_ctx=None, kr_ctx=None):
    with_ctx = kv_ctx is not None
    q_block = min(Q_BLOCK, grp.seq_len)
    nq = grp.seq_len // q_block
    n_seq = ATTN_SEQS if nq == 1 and grp.batch % ATTN_SEQS == 0 else 1
    seq = lambda rows, width: pl.BlockSpec((n_seq * rows, width), lambda b, i: (b, 0))
    place = np.zeros((QK_ROPE, HEAD_LANES), np.float32)
    place[np.arange(QK_ROPE), QK_NOPE + np.arange(QK_ROPE)] = 1.0
    in_specs = [pl.BlockSpec((n_seq * q_block, Q_COLS), lambda b, i: (b * nq + i, 0)),
                seq(grp.seq_len, KV_COLS), seq(grp.seq_len, QK_ROPE), _resident(place.shape)]
    args = [q, kv, kr, jnp.asarray(place, BF16)]
    if with_ctx:
        past = kv_ctx.shape[0] // grp.batch
        in_specs += [seq(past, KV_COLS), seq(past, QK_ROPE)]
        args += [kv_ctx, kr_ctx]
    return pl.pallas_call(
        functools.partial(_attn_kernel, with_ctx, n_seq),
        grid=(grp.batch // n_seq, nq),
        in_specs=in_specs,
        out_specs=pl.BlockSpec((n_seq * q_block, ATTN_OUT), lambda b, i: (b * nq + i, 0)),
        out_shape=jax.ShapeDtypeStruct((grp.tokens, ATTN_OUT), BF16),
        compiler_params=_params(2),
        name="attention",
    )(*args)


def _softplus(z):
    return jnp.maximum(z, 0.0) + jnp.log1p(jnp.exp(-jnp.abs(z)))


def _lru_kernel(n_chunks, chunk_len, with_state, x_ref, gg_ref, cw_ref, cb_ref, wrg_ref, brg_ref,
                wig_ref, big_ref, lam_ref, hf0_ref, hb0_ref, *rest):
    if with_state:
        s_ref, hf_ref, hb_ref, xpad, a_f, u_f, a_b, u_b = rest
    else:
        s_ref, xpad, a_f, u_f, a_b, u_b = rest
    rows = chunk_len * SUBLANES
    halo = (CONV_W // 2) * SUBLANES
    sub = lax.broadcasted_iota(jnp.int32, (SUBLANES, LRU_CH), 0)
    first_chunk = (sub % n_chunks) == 0
    last_chunk = (sub % n_chunks) == n_chunks - 1

    def from_prev(v):
        return jnp.where(first_chunk, 0.0, pltpu.roll(v, 1, 0))

    def from_next(v):
        return jnp.where(last_chunk, 0.0, pltpu.roll(v, SUBLANES - 1, 0))

    xpad[halo:halo + rows, :] = x_ref[...]
    xpad[0:SUBLANES, :] = from_prev(x_ref[rows - 2 * SUBLANES:rows - SUBLANES, :])
    xpad[SUBLANES:halo, :] = from_prev(x_ref[rows - SUBLANES:rows, :])
    xpad[halo + rows:halo + rows + SUBLANES, :] = from_next(x_ref[0:SUBLANES, :])

    def gates(c, carry):
        r0 = pl.multiple_of(c * LRU_ROWS, LRU_ROWS)
        xc = cb_ref[...]
        for k in range(CONV_W):
            xc = xc + xpad[pl.ds(r0 + k * SUBLANES, LRU_ROWS), :] * cw_ref[k:k + 1, :]
        xcb = xc.astype(BF16)
        for d, (a_ref, u_ref) in enumerate(((a_f, u_f), (a_b, u_b))):
            for n in range(LRU_CH // LRU_BW):
                sl = slice(n * LRU_BW, (n + 1) * LRU_BW)
                t_r = jnp.tanh(_dot(xcb[:, sl], wrg_ref[d, n]) + brg_ref[d:d + 1, sl])
                t_i = jnp.tanh(_dot(xcb[:, sl], wig_ref[d, n]) + big_ref[d:d + 1, sl])
                half_c = (-0.5 * LRU_C) * _softplus(-lam_ref[d:d + 1, sl])
                log_a = half_c * t_r + half_c
                a = jnp.exp(log_a)
                a_ref[pl.ds(r0, LRU_ROWS), sl] = a
                one_minus_a2 = -jnp.tanh(log_a) * (a * a + 1.0)
                root = jnp.where(one_minus_a2 > 0.0, one_minus_a2 * lax.rsqrt(one_minus_a2), 0.0)
                u_ref[pl.ds(r0, LRU_ROWS), sl] = root * (0.5 * t_i + 0.5) * xc[:, sl]
        return carry

    lax.fori_loop(0, rows // LRU_ROWS, gates, 0)

    block_rows = SCAN_BLOCK * SUBLANES

    def advance(a_ref, u_ref, first_step, order, h, store):
        slab = pl.ds(pl.multiple_of(first_step * SUBLANES, block_rows), block_rows)
        a_all, u_all = a_ref[slab, :], u_ref[slab, :]
        a = [a_all[j * SUBLANES:(j + 1) * SUBLANES] for j in order]
        u = [u_all[j * SUBLANES:(j + 1) * SUBLANES] for j in order]
        a01, u01 = a[1] * a[0], a[1] * u[0] + u[1]
        a23, u23 = a[3] * a[2], a[3] * u[2] + u[3]
        a03, u03 = a23 * a01, a23 * u01 + u23
        h4 = a03 * h + u03
        if store:
            h2 = a01 * h + u01
            states = dict(zip(order, (a[0] * h + u[0], h2, a[2] * h2 + u[2], h4)))
            u_ref[slab, :] = jnp.concatenate([states[j] for j in range(SCAN_BLOCK)], axis=0)
        return h4

    def scan(hf, hb, store):
        ascending, descending = tuple(range(SCAN_BLOCK)), tuple(reversed(range(SCAN_BLOCK)))

        def block(k, carry):
            hf, hb = carry
            hf = advance(a_f, u_f, k * SCAN_BLOCK, ascending, hf, store)
            hb = advance(a_b, u_b, chunk_len - (k + 1) * SCAN_BLOCK, descending, hb, store)
            return hf, hb
        return lax.fori_loop(0, chunk_len // SCAN_BLOCK, block, (hf, hb), unroll=SCAN_UNROLL)

    hf, hb = hf0_ref[...], hb0_ref[...]
    if n_chunks > 1:
        ef, eb = scan(hf, hb, False)
        hf = jnp.where(first_chunk, hf, pltpu.roll(ef, 1, 0))
        hb = jnp.where(last_chunk, hb, pltpu.roll(eb, SUBLANES - 1, 0))
    hf, hb = scan(hf, hb, True)
    if with_state:
        hf_ref[...] = hf
        hb_ref[...] = hb

    def combine(c, carry):
        r = pl.ds(pl.multiple_of(c * LRU_ROWS, LRU_ROWS), LRU_ROWS)
        s_ref[r, :] = ((u_f[r, :] + u_b[r, :]) * gg_ref[r, :]).astype(BF16)
        return carry

    lax.fori_loop(0, rows // LRU_ROWS, combine, 0)


def _rg_lru(grp, l, lx, gg, conv_w, conv_b, w_rg, b_rg, w_ig, b_ig, lam, hf0, hb0, with_state):
    assert grp.n_chunks == 1 or not with_state
    rows = grp.chunk_len * SUBLANES
    nb = LRU_CH // LRU_BW
    tile = pl.BlockSpec((rows, LRU_CH), lambda g, c: (g, c))
    chan = lambda r: pl.BlockSpec((None, r, LRU_CH), lambda g, c: (l, 0, c))
    wblk = pl.BlockSpec((None, 2, nb, LRU_BW, LRU_BW), lambda g, c: (l, 0, c, 0, 0))
    state = pl.BlockSpec((SUBLANES, LRU_CH), lambda g, c: (g, c))
    out_specs = [tile]
    out_shape = [grp.shape_tm(LRU_WIDTH, BF16)]
    if with_state:
        out_specs += [state, state]
        out_shape += [jax.ShapeDtypeStruct((grp.n_pseudo, LRU_WIDTH), F32)] * 2
    scratch = [pltpu.VMEM((rows + (CONV_W - 1) * SUBLANES, LRU_CH), F32)] + [pltpu.VMEM((rows, LRU_CH), F32)] * 4
    return pl.pallas_call(
        functools.partial(_lru_kernel, grp.n_chunks, grp.chunk_len, with_state),
        grid=(grp.lru_groups, LRU_WIDTH // LRU_CH),
        in_specs=[tile, tile, chan(CONV_W), chan(1), wblk, chan(2), wblk, chan(2), chan(2), state, state],
        out_specs=out_specs,
        out_shape=out_shape,
        scratch_shapes=scratch,
        compiler_params=_params(2),
        name="rg_lru",
    )(lx, gg, conv_w, conv_b, w_rg, b_rg, w_ig, b_ig, lam, hf0, hb0)


def _merge_kernel(o_ref, s_ref, gates_ref, x_ref, mod_ref, perm_ref, gpost_ref, wom_ref, wol_ref, wout_ref, y_ref):
    y_mla = _dot(_flat(o_ref[...]), wom_ref[...])
    s = _flat(_seq_major_tile(perm_ref[...], s_ref[...]))
    y_lru = _dot(s, wol_ref[...])
    gates = _flat(gates_ref[...])
    z = gates[:, :D_MODEL] * y_mla + gates[:, D_MODEL:] * y_lru
    y = _dot(z.astype(BF16), wout_ref[...])
    g1 = mod_ref[:, :, 2 * D_MODEL:3 * D_MODEL]
    y_ref[...] = x_ref[...] + g1 * _tiled(_rms(y, gpost_ref[...]))


def _merge(grp, l, o, s, gates, x, mod, perm_t, g_post, w_o_mla, w_o_lru, w_out):
    return pl.pallas_call(
        _merge_kernel,
        grid=grp.grid,
        in_specs=[grp.seq_major(ATTN_OUT), grp.time_major(LRU_WIDTH), grp.seq_major(2 * D_MODEL),
                  grp.seq_major(D_MODEL), grp.per_pseudo(N_MOD * D_MODEL, l), _resident(perm_t.shape),
                  _resident((1, D_MODEL), l), _resident(w_o_mla.shape[1:], l), _resident(w_o_lru.shape[1:], l),
                  _resident(w_out.shape[1:], l)],
        out_specs=grp.seq_major(D_MODEL),
        out_shape=grp.shape3(D_MODEL, F32),
        compiler_params=_params(2),
        name="merge",
    )(o, s, gates, x, mod, perm_t, g_post, w_o_mla, w_o_lru, w_out)


def _ffn_kernel(x_ref, mod_ref, gpre_ref, gpost_ref, w1_ref, w2_ref, y_ref, act_ref):
    x = x_ref[...]
    sh2, sc2, g2 = (mod_ref[:, :, k * D_MODEL:(k + 1) * D_MODEL] for k in (3, 4, 5))
    hb = _flat((_rms(x, gpre_ref[...]) * (1.0 + sc2) + sh2).astype(BF16))
    for c in range(FF_HIDDEN // FFN_CHUNK):
        lo = c * FFN_CHUNK
        g = _dot(hb, w1_ref[:, lo:lo + FFN_CHUNK])
        u = _dot(hb, w1_ref[:, FF_HIDDEN + lo:FF_HIDDEN + lo + FFN_CHUNK])
        act_ref[:, lo:lo + FFN_CHUNK] = (g * _sigmoid(g) * u).astype(BF16)
    y = _dot(act_ref[...], w2_ref[...])
    y_ref[...] = x + g2 * _tiled(_rms(y, gpost_ref[...]))


def _ffn(grp, l, x, mod, g_pre, g_post, w1, w2):
    assert grp.chunk_len % FFN_STEPS == 0
    return pl.pallas_call(
        _ffn_kernel,
        grid=(grp.lru_groups, grp.chunk_len // FFN_STEPS),
        in_specs=[grp.seq_major(D_MODEL, FFN_STEPS), grp.per_pseudo(N_MOD * D_MODEL, l),
                  _resident((1, D_MODEL), l), _resident((1, D_MODEL), l), _resident(w1.shape[1:], l),
                  _resident(w2.shape[1:], l)],
        out_specs=grp.seq_major(D_MODEL, FFN_STEPS),
        out_shape=grp.shape3(D_MODEL, F32),
        scratch_shapes=[pltpu.VMEM((SUBLANES * FFN_STEPS, FF_HIDDEN), BF16)],
        compiler_params=_params(2),
        name="ffn",
    )(x, mod, g_pre, g_post, w1, w2)


def _rot_partner(w):
    half = QK_ROPE // 2
    return jnp.concatenate([-w[..., half:], w[..., :half]], axis=-1)


def _rope_tables(grp):
    n_tokens = grp.seq_len
    rows = n_tokens // GRID_W
    row = jnp.repeat(jnp.arange(rows, dtype=F32), GRID_W)
    col = jnp.tile(jnp.arange(GRID_W, dtype=F32), rows)
    n_freq = QK_ROPE // 4
    inv = ROPE_THETA ** (-jnp.arange(n_freq, dtype=F32) / n_freq)
    ang = jnp.concatenate([row[:, None] * inv, col[:, None] * inv], axis=-1)
    cos, sin = jnp.cos(ang), jnp.sin(ang)
    cos2, sin2 = jnp.concatenate([cos, cos], -1), jnp.concatenate([sin, sin], -1)
    ktab = jnp.concatenate([cos2, sin2, jnp.zeros((n_tokens, LANES - 2 * QK_ROPE), F32)], -1)
    scale = QK_DIM ** -0.5
    tail = jnp.zeros((n_tokens, HEAD_LANES - QK_DIM), F32)
    qcos = jnp.concatenate([jnp.ones((n_tokens, QK_NOPE), F32), cos2, tail], -1) * scale
    qsin = jnp.concatenate([jnp.zeros((n_tokens, QK_NOPE), F32), sin2, tail], -1) * scale

    def by_pseudo(tab):
        tab = tab.reshape(grp.n_chunks, grp.chunk_len, tab.shape[-1])
        return jnp.tile(tab, (SUBLANES // grp.n_chunks, 1, 1))

    return by_pseudo(qcos), by_pseudo(qsin), by_pseudo(ktab)


def kernel(x_prompt, x_sample, cache_ckv, cache_krope, state_lru_fwd, state_lru_bwd, c, c_ctx, w_mod, b_mod, g_pre_mix, g_post_mix, g_pre_ffn, g_post_ffn, w_in, g_q, w_q_up, g_kv, w_kv_up, w_o_mla, conv_w, conv_b, w_rg, b_rg, w_ig, b_ig, lru_lambda, w_o_lru, w_out, w_ffn_in, w_ffn_out):
    batch, seq = x_prompt.shape[:2]
    dec_batch, dec_seq = x_sample.shape[:2]
    past = cache_ckv.shape[2]

    w_in_r = _w_in_layout(w_in)
    wq = w_q_up.reshape(DEPTH, Q_LORA, N_HEADS, QK_DIM)
    q_tail = jnp.zeros((DEPTH, Q_LORA, N_HEADS, HEAD_LANES - QK_DIM), F32)
    q_main = jnp.concatenate([wq, q_tail], axis=-1)
    q_partner = jnp.concatenate([jnp.zeros_like(wq[..., :QK_NOPE]), _rot_partner(wq[..., QK_NOPE:]), q_tail], axis=-1)
    w_q_r = jnp.concatenate([q_main.reshape(DEPTH, Q_LORA, Q_COLS),
                             q_partner.reshape(DEPTH, Q_LORA, Q_COLS)], axis=-1).astype(BF16)
    w_kv_r = w_kv_up.astype(BF16)
    wo = w_o_mla.reshape(DEPTH, N_HEADS, V_DIM, D_MODEL)
    w_o_mla_b = jnp.concatenate([jnp.zeros((DEPTH, N_HEADS, HEAD_LANES - V_DIM, D_MODEL), F32), wo],
                                axis=2).reshape(DEPTH, ATTN_OUT, D_MODEL).astype(BF16)
    w_o_lru_b, w_out_b = w_o_lru.astype(BF16), w_out.astype(BF16)
    w_rg_b, w_ig_b = (0.5 * w_rg).astype(BF16), (0.5 * w_ig).astype(BF16)
    b_rg, b_ig = 0.5 * b_rg, 0.5 * b_ig
    w1_b, w2_b = w_ffn_in.astype(BF16), w_ffn_out.astype(BF16)
    row = lambda v: v.reshape(DEPTH, 1, -1)
    g_pre_mix, g_post_mix, g_pre_ffn, g_post_ffn = map(row, (g_pre_mix, g_post_mix, g_pre_ffn, g_post_ffn))
    g_q, g_kv, conv_b = row(g_q), row(g_kv), row(conv_b)
    perm = _to_time_major()
    perm_b, perm_t_b = jnp.asarray(perm, BF16), jnp.asarray(perm.T, BF16)

    mod_all = _modulation(jnp.concatenate([c_ctx[None, :], c], axis=0), w_mod, b_mod)

    def run_group(grp, x, mod_rows, layer_inputs):
        x = x.reshape(grp.n_pseudo, grp.chunk_len, D_MODEL)
        rope_tabs = _rope_tables(grp) if grp.latent else None
        flat = lambda a: a.reshape(grp.tokens, a.shape[-1])
        mod = mod_all[:, mod_rows][:, :, None, :]
        per_layer = []
        for l in range(DEPTH):
            q, ckv, kv, kr, lx, gg, gates = _in_proj(grp, l, x, mod, perm_b, g_pre_mix, w_in_r, g_q, w_q_r,
                                                     g_kv, w_kv_r, rope_tabs)
            hf0, hb0, kv_ctx, kr_ctx = layer_inputs(l)
            o = _attention(grp, flat(q), flat(kv), flat(kr), kv_ctx, kr_ctx)
            lru = _rg_lru(grp, l, lx, gg, conv_w, conv_b, w_rg_b, b_rg, w_ig_b, b_ig, lru_lambda,
                          hf0, hb0, not grp.latent)
            o = o.reshape(grp.n_pseudo, grp.chunk_len, ATTN_OUT)
            x = _merge(grp, l, o, lru[0], gates, x, mod, perm_t_b, g_post_mix, w_o_mla_b, w_o_lru_b, w_out_b)
            x = _ffn(grp, l, x, mod, g_pre_ffn, g_post_ffn, w1_b, w2_b)
            per_layer.append((ckv, kr) + tuple(lru[1:]))
        return x, per_layer

    ctx = _Group(batch, seq, latent=False)
    zeros_state = jnp.zeros((batch, LRU_WIDTH), F32)
    y_prompt, ctx_layers = run_group(ctx, x_prompt, np.zeros(ctx.n_pseudo, np.int32),
                                     lambda l: (zeros_state, zeros_state, None, None))
    stack = lambda k, shape: jnp.stack([lay[k].reshape(shape) for lay in ctx_layers], axis=1)
    new_ckv = stack(0, (batch, seq, KV_LORA))
    new_krope = stack(1, (batch, seq, QK_ROPE))
    new_lru_fwd = stack(2, (batch, LRU_WIDTH))
    new_lru_bwd = stack(3, (batch, LRU_WIDTH))

    lat = _Group(dec_batch, dec_seq, latent=True)

    def latent_inputs(l):
        per_chunk = lambda s: jnp.repeat(s[:, l], lat.n_chunks, axis=0)
        kv_ctx = _kv_up(l, cache_ckv[:, l].reshape(dec_batch * past, KV_LORA), w_kv_r, past)
        return (per_chunk(state_lru_fwd), per_chunk(state_lru_bwd), kv_ctx,
                cache_krope[:, l].reshape(dec_batch * past, QK_ROPE))

    y_sample, _ = run_group(lat, x_sample, 1 + np.arange(lat.n_pseudo) // lat.n_chunks, latent_inputs)
    return (y_prompt.reshape(batch, seq, D_MODEL), y_sample.reshape(dec_batch, dec_seq, D_MODEL),
            new_ckv, new_krope, new_lru_fwd, new_lru_bwd)
```

```python
import functools

import numpy as np

import jax
import jax.numpy as jnp
from jax import lax
from jax.experimental import pallas as pl
from jax.experimental.pallas import tpu as pltpu

D_MODEL = 1024
DEPTH = 2
GRID_W = 64
N_HEADS = 8
QK_NOPE = 64
QK_ROPE = 32
QK_DIM = QK_NOPE + QK_ROPE
V_DIM = 64
Q_LORA = 384
KV_LORA = 256
ROPE_THETA = 10000.0
LRU_WIDTH = 1024
LRU_BLOCKS = 8
LRU_BW = LRU_WIDTH // LRU_BLOCKS
CONV_W = 4
LRU_C = 8.0
FF_HIDDEN = 2816
N_MOD = 6
EPS = 1e-6

SUBLANES = 8
LANES = 128
VMEM_LIMIT_BYTES = 56 * 1024 * 1024

C_Q = 0
C_KV = C_Q + Q_LORA
C_LX = C_KV + KV_LORA
C_LG = C_LX + LRU_WIDTH
C_MG = C_LG + LRU_WIDTH
C_KR = C_MG + 2 * D_MODEL
IN_COLS_PADDED = C_KR + LANES
HEAD_LANES = LANES
assert QK_DIM <= HEAD_LANES and QK_NOPE + V_DIM == HEAD_LANES
Q_COLS = N_HEADS * HEAD_LANES
KV_COLS = N_HEADS * HEAD_LANES
ATTN_OUT = N_HEADS * HEAD_LANES

STEPS = 64
TILE_ROWS = SUBLANES * STEPS
PERM_STEPS = 32
PERM_ROWS = SUBLANES * PERM_STEPS
assert STEPS % PERM_STEPS == 0
Q_BLOCK = 1024
ATTN_SEQS = 8
FFN_STEPS = 128
FFN_CHUNK = 256
LRU_CH = 256
LRU_ROWS = 512
SCAN_BLOCK = 4
SCAN_UNROLL = 2
MOD_TILE = 1536

F32 = jnp.float32
BF16 = jnp.bfloat16


def _params(n_grid_dims):
    return pltpu.CompilerParams(dimension_semantics=("arbitrary",) * n_grid_dims,
                                vmem_limit_bytes=VMEM_LIMIT_BYTES)


def _resident(shape, layer=None):
    zeros = (0,) * len(shape)
    if layer is None:
        return pl.BlockSpec(shape, lambda *_: zeros, pipeline_mode=pl.Buffered(1))
    return pl.BlockSpec((None,) + tuple(shape), lambda *_: (layer,) + zeros, pipeline_mode=pl.Buffered(1))


def _rms(x, g):
    return x * lax.rsqrt(jnp.mean(x * x, axis=-1, keepdims=True) + EPS) * g


def _dot(a, b):
    return jnp.dot(a, b, preferred_element_type=F32)


def _dot_t(a, b):
    return lax.dot_general(a, b, (((1,), (1,)), ((), ())), preferred_element_type=F32)


def _flat(x3):
    return x3.reshape(x3.shape[0] * x3.shape[1], x3.shape[-1])


def _tiled(x2):
    return x2.reshape(SUBLANES, x2.shape[0] // SUBLANES, x2.shape[-1])


def _sigmoid(x):
    return 0.5 * jnp.tanh(0.5 * x) + 0.5


def _to_time_major():
    r = np.arange(PERM_ROWS)
    perm = np.zeros((PERM_ROWS, PERM_ROWS), np.float32)
    perm[r, (r % SUBLANES) * PERM_STEPS + r // SUBLANES] = 1.0
    return perm


def _time_major_rows(perm, x3):
    parts = [_dot(perm, _flat(x3[:, t:t + PERM_STEPS, :])).astype(BF16) for t in range(0, x3.shape[1], PERM_STEPS)]
    return jnp.concatenate(parts, axis=0)


def _seq_major_tile(perm_t, x2):
    parts = [_tiled(_dot(perm_t, x2[r:r + PERM_ROWS, :]).astype(BF16)) for r in range(0, x2.shape[0], PERM_ROWS)]
    return jnp.concatenate(parts, axis=1)


class _Group:
    def __init__(self, batch, seq_len, latent):
        self.batch, self.seq_len, self.latent = batch, seq_len, latent
        self.tokens = batch * seq_len
        self.n_chunks = max(1, SUBLANES // batch)
        self.n_pseudo = batch * self.n_chunks
        assert self.n_chunks in (1, 2) and self.n_pseudo % SUBLANES == 0
        self.chunk_len = seq_len // self.n_chunks
        assert self.chunk_len % STEPS == 0 and seq_len % min(Q_BLOCK, seq_len) == 0
        self.lru_groups = self.n_pseudo // SUBLANES
        self.tiles_per_chunk = self.chunk_len // STEPS
        self.grid = (self.lru_groups, self.tiles_per_chunk)

    def seq_major(self, width, steps=STEPS):
        return pl.BlockSpec((SUBLANES, steps, width), lambda g, i: (g, i, 0))

    def time_major(self, width):
        return pl.BlockSpec((TILE_ROWS, width), lambda g, i: (g * self.tiles_per_chunk + i, 0))

    def per_pseudo(self, width, layer):
        return pl.BlockSpec((None, SUBLANES, 1, width), lambda g, i: (layer, g, 0, 0))

    def by_position(self, width):
        return pl.BlockSpec((SUBLANES, STEPS, width), lambda g, i: (0, i, 0))

    def shape3(self, width, dtype):
        return jax.ShapeDtypeStruct((self.n_pseudo, self.chunk_len, width), dtype)

    def shape_tm(self, width, dtype):
        return jax.ShapeDtypeStruct((self.lru_groups * self.chunk_len * SUBLANES, width), dtype)


def _mod_kernel(c_ref, w_ref, b_ref, o_ref):
    c = c_ref[...]
    a = (c * _sigmoid(c)).astype(BF16)
    o_ref[...] = _dot(a, w_ref[...].astype(BF16)) + b_ref[...]


def _modulation(cond, w_mod, b_mod):
    n = N_MOD * D_MODEL
    rows = cond.shape[0]
    return pl.pallas_call(
        _mod_kernel,
        grid=(DEPTH, n // MOD_TILE),
        in_specs=[pl.BlockSpec((rows, D_MODEL), lambda l, j: (0, 0)),
                  pl.BlockSpec((None, D_MODEL, MOD_TILE), lambda l, j: (l, 0, j)),
                  pl.BlockSpec((None, 1, MOD_TILE), lambda l, j: (l, 0, j))],
        out_specs=pl.BlockSpec((None, rows, MOD_TILE), lambda l, j: (l, 0, j)),
        out_shape=jax.ShapeDtypeStruct((DEPTH, rows, n), F32),
        compiler_params=_params(2),
        name="modulation",
    )(cond, w_mod, b_mod.reshape(DEPTH, 1, n))


def _w_in_layout_kernel(w_ref, o_ref):
    off_kr = Q_LORA + KV_LORA
    lane = lax.broadcasted_iota(jnp.int32, (1, LANES), 1)
    half = QK_ROPE // 2
    o_ref[:, C_Q:C_LX] = w_ref[:, :off_kr].astype(BF16)
    o_ref[:, C_LX:C_KR] = w_ref[:, off_kr + QK_ROPE:].astype(BF16)
    tile = w_ref[:, off_kr:off_kr + LANES]
    partner = jnp.where(lane < half, -pltpu.roll(tile, LANES - half, 1), pltpu.roll(tile, half, 1))
    seg = jnp.where(lane < QK_ROPE, tile, jnp.where(lane < 2 * QK_ROPE, pltpu.roll(partner, QK_ROPE, 1), 0.0))
    o_ref[:, C_KR:IN_COLS_PADDED] = seg.astype(BF16)


def _w_in_layout(w_in):
    rows = 256
    in_cols = w_in.shape[-1]
    return pl.pallas_call(
        _w_in_layout_kernel,
        grid=(DEPTH, D_MODEL // rows),
        in_specs=[pl.BlockSpec((None, rows, in_cols), lambda l, i: (l, i, 0))],
        out_specs=pl.BlockSpec((None, rows, IN_COLS_PADDED), lambda l, i: (l, i, 0)),
        out_shape=jax.ShapeDtypeStruct((DEPTH, D_MODEL, IN_COLS_PADDED), BF16),
        compiler_params=_params(2),
        name="w_in_layout",
    )(w_in)


def _in_proj_kernel(latent, x_ref, mod_ref, perm_ref, gpre_ref, win_ref, gq_ref, wq_ref, gkv_ref, wkv_ref, *rest):
    if latent:
        cosq_ref, sinq_ref, ktab_ref = rest[:3]
        rest = rest[3:]
    q_ref, ckv_ref, kv_ref, kr_ref, lx_ref, gg_ref, gates_ref = rest
    sh1, sc1 = mod_ref[:, :, 0:D_MODEL], mod_ref[:, :, D_MODEL:2 * D_MODEL]
    hb3 = (_rms(x_ref[...], gpre_ref[...]) * (1.0 + sc1) + sh1).astype(BF16)
    hb = _flat(hb3)

    def proj(lhs, lo, hi):
        return _dot(lhs, win_ref[:, lo:hi])

    qn = _rms(proj(hb, C_Q, C_KV), gq_ref[...]).astype(BF16)
    qf = _dot(qn, wq_ref[...])
    if latent:
        cos, sin = _flat(cosq_ref[...]), _flat(sinq_ref[...])
        for h in range(N_HEADS):
            lanes = slice(h * HEAD_LANES, (h + 1) * HEAD_LANES)
            partner = slice(Q_COLS + h * HEAD_LANES, Q_COLS + (h + 1) * HEAD_LANES)
            q_ref[:, :, lanes] = _tiled((qf[:, lanes] * cos + qf[:, partner] * sin).astype(BF16))
    else:
        q_ref[...] = _tiled((qf * QK_DIM ** -0.5).astype(BF16))

    ckv = _rms(proj(hb, C_KV, C_LX), gkv_ref[...])
    ckv_ref[...] = _tiled(ckv)
    kv_ref[...] = _tiled(_dot(ckv.astype(BF16), wkv_ref[...]).astype(BF16))

    kr = proj(hb, C_KR, IN_COLS_PADDED)
    if latent:
        pr = kr * _flat(ktab_ref[...])
        kr = pr + pltpu.roll(pr, LANES - QK_ROPE, 1)
    kr_ref[...] = _tiled(kr[:, :QK_ROPE])

    gates_ref[...] = _tiled(_sigmoid(proj(hb, C_MG, C_KR)))

    hb_tm = _time_major_rows(perm_ref[...], hb3)
    lx_ref[...] = proj(hb_tm, C_LX, C_LG)
    gg_ref[...] = jax.nn.gelu(proj(hb_tm, C_LG, C_MG))


def _in_proj(grp, l, x, mod, perm, g_pre, w_in, g_q, w_q, g_kv, w_kv, rope_tabs):
    q_cols = w_q.shape[-1] if grp.latent else Q_COLS
    in_specs = [grp.seq_major(D_MODEL), grp.per_pseudo(N_MOD * D_MODEL, l), _resident(perm.shape),
                _resident((1, D_MODEL), l), _resident(w_in.shape[1:], l), _resident((1, Q_LORA), l),
                _resident((Q_LORA, q_cols), l), _resident((1, KV_LORA), l), _resident(w_kv.shape[1:], l)]
    args = [x, mod, perm, g_pre, w_in, g_q, w_q, g_kv, w_kv]
    if grp.latent:
        in_specs += [grp.by_position(HEAD_LANES), grp.by_position(HEAD_LANES), grp.by_position(LANES)]
        args += list(rope_tabs)
    return pl.pallas_call(
        functools.partial(_in_proj_kernel, grp.latent),
        grid=grp.grid,
        in_specs=in_specs,
        out_specs=[grp.seq_major(Q_COLS), grp.seq_major(KV_LORA), grp.seq_major(KV_COLS), grp.seq_major(QK_ROPE),
                   grp.time_major(LRU_WIDTH), grp.time_major(LRU_WIDTH), grp.seq_major(2 * D_MODEL)],
        out_shape=[grp.shape3(Q_COLS, BF16), grp.shape3(KV_LORA, F32), grp.shape3(KV_COLS, BF16),
                   grp.shape3(QK_ROPE, F32), grp.shape_tm(LRU_WIDTH, F32), grp.shape_tm(LRU_WIDTH, F32),
                   grp.shape3(2 * D_MODEL, F32)],
        compiler_params=_params(2),
        name="in_proj",
    )(*args)


def _kv_up_kernel(c_ref, w_ref, o_ref):
    o_ref[...] = _dot(c_ref[...].astype(BF16), w_ref[...]).astype(BF16)


def _kv_up(l, ckv, w_kv, rows_per_step):
    n = ckv.shape[0]
    return pl.pallas_call(
        _kv_up_kernel,
        grid=(n // rows_per_step,),
        in_specs=[pl.BlockSpec((rows_per_step, KV_LORA), lambda i: (i, 0)), _resident(w_kv.shape[1:], l)],
        out_specs=pl.BlockSpec((rows_per_step, KV_COLS), lambda i: (i, 0)),
        out_shape=jax.ShapeDtypeStruct((n, KV_COLS), BF16),
        compiler_params=_params(1),
        name="kv_up_ctx",
    )(ckv, w_kv)


def _attn_kernel(with_ctx, n_seq, q_ref, kv_ref, kr_ref, place_ref, *rest):
    if with_ctx:
        kvc_ref, krc_ref, o_ref = rest
    else:
        (o_ref,) = rest
    is_nope = lax.broadcasted_iota(jnp.int32, (1, HEAD_LANES), 1) < QK_NOPE

    def keys_of(kv, kr, rows):
        kr_placed = _dot(kr[rows, :].astype(BF16), place_ref[...]).astype(BF16)
        return lambda h: jnp.where(is_nope, kv[rows, h * HEAD_LANES:(h + 1) * HEAD_LANES], kr_placed)

    share = lambda ref, j: slice(j * (ref.shape[0] // n_seq), (j + 1) * (ref.shape[0] // n_seq))
    for j in range(n_seq):
        q_rows, k_rows = share(q_ref, j), share(kv_ref, j)
        keys = keys_of(kv_ref, kr_ref, k_rows)
        if with_ctx:
            c_rows = share(kvc_ref, j)
            keys_ctx = keys_of(kvc_ref, krc_ref, c_rows)
        for h in range(N_HEADS):
            lanes = slice(h * HEAD_LANES, (h + 1) * HEAD_LANES)
            q = q_ref[q_rows, lanes]
            s = _dot_t(q, keys(h))
            m = jnp.max(s, axis=-1, keepdims=True)
            if with_ctx:
                sc = _dot_t(q, keys_ctx(h))
                m = jnp.maximum(m, jnp.max(sc, axis=-1, keepdims=True))
            p = jnp.exp(s - m)
            den = jnp.sum(p, axis=-1, keepdims=True)
            o = _dot(p.astype(BF16), kv_ref[k_rows, lanes])
            if with_ctx:
                pc = jnp.exp(sc - m)
                den = den + jnp.sum(pc, axis=-1, keepdims=True)
                o = o + _dot(pc.astype(BF16), kvc_ref[c_rows, lanes])
            o_ref[q_rows, lanes] = (o / den).astype(BF16)


def _attention(grp, q, kv, kr, kv_ctx=None, kr_ctx=None):
    with_ctx = kv_ctx is not None
    q_block = min(Q_BLOCK, grp.seq_len)
    nq = grp.seq_len // q_block
    n_seq = ATTN_SEQS if nq == 1 and grp.batch % ATTN_SEQS == 0 else 1
    seq = lambda rows, width: pl.BlockSpec((n_seq * rows, width), lambda b, i: (b, 0))
    place = np.zeros((QK_ROPE, HEAD_LANES), np.float32)
    place[np.arange(QK_ROPE), QK_NOPE + np.arange(QK_ROPE)] = 1.0
    in_specs = [pl.BlockSpec((n_seq * q_block, Q_COLS), lambda b, i: (b * nq + i, 0)),
                seq(grp.seq_len, KV_COLS), seq(grp.seq_len, QK_ROPE), _resident(place.shape)]
    args = [q, kv, kr, jnp.asarray(place, BF16)]
    if with_ctx:
        past = kv_ctx.shape[0] // grp.batch
        in_specs += [seq(past, KV_COLS), seq(past, QK_ROPE)]
        args += [kv_ctx, kr_ctx]
    return pl.pallas_call(
        functools.partial(_attn_kernel, with_ctx, n_seq),
        grid=(grp.batch // n_seq, nq),
        in_specs=in_specs,
        out_specs=pl.BlockSpec((n_seq * q_block, ATTN_OUT), lambda b, i: (b * nq + i, 0)),
        out_shape=jax.ShapeDtypeStruct((grp.tokens, ATTN_OUT), BF16),
        compiler_params=_params(2),
        name="attention",
    )(*args)


def _softplus(z):
    return jnp.maximum(z, 0.0) + jnp.log1p(jnp.exp(-jnp.abs(z)))


def _lru_kernel(n_chunks, chunk_len, with_state, x_ref, gg_ref, cw_ref, cb_ref, wrg_ref, brg_ref,
                wig_ref, big_ref, lam_ref, hf0_ref, hb0_ref, *rest):
    if with_state:
        s_ref, hf_ref, hb_ref, xpad, a_f, u_f, a_b, u_b = rest
    else:
        s_ref, xpad, a_f, u_f, a_b, u_b = rest
    rows = chunk_len * SUBLANES
    halo = (CONV_W // 2) * SUBLANES
    sub = lax.broadcasted_iota(jnp.int32, (SUBLANES, LRU_CH), 0)
    first_chunk = (sub % n_chunks) == 0
    last_chunk = (sub % n_chunks) == n_chunks - 1

    def from_prev(v):
        return jnp.where(first_chunk, 0.0, pltpu.roll(v, 1, 0))

    def from_next(v):
        return jnp.where(last_chunk, 0.0, pltpu.roll(v, SUBLANES - 1, 0))

    xpad[halo:halo + rows, :] = x_ref[...]
    xpad[0:SUBLANES, :] = from_prev(x_ref[rows - 2 * SUBLANES:rows - SUBLANES, :])
    xpad[SUBLANES:halo, :] = from_prev(x_ref[rows - SUBLANES:rows, :])
    xpad[halo + rows:halo + rows + SUBLANES, :] = from_next(x_ref[0:SUBLANES, :])

    def gates(c, carry):
        r0 = pl.multiple_of(c * LRU_ROWS, LRU_ROWS)
        xc = cb_ref[...]
        for k in range(CONV_W):
            xc = xc + xpad[pl.ds(r0 + k * SUBLANES, LRU_ROWS), :] * cw_ref[k:k + 1, :]
        xcb = xc.astype(BF16)
        for d, (a_ref, u_ref) in enumerate(((a_f, u_f), (a_b, u_b))):
            for n in range(LRU_CH // LRU_BW):
                sl = slice(n * LRU_BW, (n + 1) * LRU_BW)
                t_r = jnp.tanh(_dot(xcb[:, sl], wrg_ref[d, n]) + brg_ref[d:d + 1, sl])
                t_i = jnp.tanh(_dot(xcb[:, sl], wig_ref[d, n]) + big_ref[d:d + 1, sl])
                half_c = (-0.5 * LRU_C) * _softplus(-lam_ref[d:d + 1, sl])
                log_a = half_c * t_r + half_c
                a = jnp.exp(log_a)
                a_ref[pl.ds(r0, LRU_ROWS), sl] = a
                one_minus_a2 = -jnp.tanh(log_a) * (a * a + 1.0)
                root = jnp.where(one_minus_a2 > 0.0, one_minus_a2 * lax.rsqrt(one_minus_a2), 0.0)
                u_ref[pl.ds(r0, LRU_ROWS), sl] = root * (0.5 * t_i + 0.5) * xc[:, sl]
        return carry

    lax.fori_loop(0, rows // LRU_ROWS, gates, 0)

    block_rows = SCAN_BLOCK * SUBLANES

    def advance(a_ref, u_ref, first_step, order, h, store):
        slab = pl.ds(pl.multiple_of(first_step * SUBLANES, block_rows), block_rows)
        a_all, u_all = a_ref[slab, :], u_ref[slab, :]
        a = [a_all[j * SUBLANES:(j + 1) * SUBLANES] for j in order]
        u = [u_all[j * SUBLANES:(j + 1) * SUBLANES] for j in order]
        a01, u01 = a[1] * a[0], a[1] * u[0] + u[1]
        a23, u23 = a[3] * a[2], a[3] * u[2] + u[3]
        a03, u03 = a23 * a01, a23 * u01 + u23
        h4 = a03 * h + u03
        if store:
            h2 = a01 * h + u01
            states = dict(zip(order, (a[0] * h + u[0], h2, a[2] * h2 + u[2], h4)))
            u_ref[slab, :] = jnp.concatenate([states[j] for j in range(SCAN_BLOCK)], axis=0)
        return h4

    def scan(hf, hb, store):
        ascending, descending = tuple(range(SCAN_BLOCK)), tuple(reversed(range(SCAN_BLOCK)))

        def block(k, carry):
            hf, hb = carry
            hf = advance(a_f, u_f, k * SCAN_BLOCK, ascending, hf, store)
            hb = advance(a_b, u_b, chunk_len - (k + 1) * SCAN_BLOCK, descending, hb, store)
            return hf, hb
        return lax.fori_loop(0, chunk_len // SCAN_BLOCK, block, (hf, hb), unroll=SCAN_UNROLL)

    hf, hb = hf0_ref[...], hb0_ref[...]
    if n_chunks > 1:
        ef, eb = scan(hf, hb, False)
        hf = jnp.where(first_chunk, hf, pltpu.roll(ef, 1, 0))
        hb = jnp.where(last_chunk, hb, pltpu.roll(eb, SUBLANES - 1, 0))
    hf, hb = scan(hf, hb, True)
    if with_state:
        hf_ref[...] = hf
        hb_ref[...] = hb

    def combine(c, carry):
        r = pl.ds(pl.multiple_of(c * LRU_ROWS, LRU_ROWS), LRU_ROWS)
        s_ref[r, :] = ((u_f[r, :] + u_b[r, :]) * gg_ref[r, :]).astype(BF16)
        return carry

    lax.fori_loop(0, rows // LRU_ROWS, combine, 0)


def _rg_lru(grp, l, lx, gg, conv_w, conv_b, w_rg, b_rg, w_ig, b_ig, lam, hf0, hb0, with_state):
    assert grp.n_chunks == 1 or not with_state
    rows = grp.chunk_len * SUBLANES
    nb = LRU_CH // LRU_BW
    tile = pl.BlockSpec((rows, LRU_CH), lambda g, c: (g, c))
    chan = lambda r: pl.BlockSpec((None, r, LRU_CH), lambda g, c: (l, 0, c))
    wblk = pl.BlockSpec((None, 2, nb, LRU_BW, LRU_BW), lambda g, c: (l, 0, c, 0, 0))
    state = pl.BlockSpec((SUBLANES, LRU_CH), lambda g, c: (g, c))
    out_specs = [tile]
    out_shape = [grp.shape_tm(LRU_WIDTH, BF16)]
    if with_state:
        out_specs += [state, state]
        out_shape += [jax.ShapeDtypeStruct((grp.n_pseudo, LRU_WIDTH), F32)] * 2
    scratch = [pltpu.VMEM((rows + (CONV_W - 1) * SUBLANES, LRU_CH), F32)] + [pltpu.VMEM((rows, LRU_CH), F32)] * 4
    return pl.pallas_call(
        functools.partial(_lru_kernel, grp.n_chunks, grp.chunk_len, with_state),
        grid=(grp.lru_groups, LRU_WIDTH // LRU_CH),
        in_specs=[tile, tile, chan(CONV_W), chan(1), wblk, chan(2), wblk, chan(2), chan(2), state, state],
        out_specs=out_specs,
        out_shape=out_shape,
        scratch_shapes=scratch,
        compiler_params=_params(2),
        name="rg_lru",
    )(lx, gg, conv_w, conv_b, w_rg, b_rg, w_ig, b_ig, lam, hf0, hb0)


def _merge_kernel(o_ref, s_ref, gates_ref, x_ref, mod_ref, perm_ref, gpost_ref, wom_ref, wol_ref, wout_ref, y_ref):
    y_mla = _dot(_flat(o_ref[...]), wom_ref[...])
    s = _flat(_seq_major_tile(perm_ref[...], s_ref[...]))
    y_lru = _dot(s, wol_ref[...])
    gates = _flat(gates_ref[...])
    z = gates[:, :D_MODEL] * y_mla + gates[:, D_MODEL:] * y_lru
    y = _dot(z.astype(BF16), wout_ref[...])
    g1 = mod_ref[:, :, 2 * D_MODEL:3 * D_MODEL]
    y_ref[...] = x_ref[...] + g1 * _tiled(_rms(y, gpost_ref[...]))


def _merge(grp, l, o, s, gates, x, mod, perm_t, g_post, w_o_mla, w_o_lru, w_out):
    return pl.pallas_call(
        _merge_kernel,
        grid=grp.grid,
        in_specs=[grp.seq_major(ATTN_OUT), grp.time_major(LRU_WIDTH), grp.seq_major(2 * D_MODEL),
                  grp.seq_major(D_MODEL), grp.per_pseudo(N_MOD * D_MODEL, l), _resident(perm_t.shape),
                  _resident((1, D_MODEL), l), _resident(w_o_mla.shape[1:], l), _resident(w_o_lru.shape[1:], l),
                  _resident(w_out.shape[1:], l)],
        out_specs=grp.seq_major(D_MODEL),
        out_shape=grp.shape3(D_MODEL, F32),
        compiler_params=_params(2),
        name="merge",
    )(o, s, gates, x, mod, perm_t, g_post, w_o_mla, w_o_lru, w_out)


def _ffn_kernel(x_ref, mod_ref, gpre_ref, gpost_ref, w1_ref, w2_ref, y_ref, act_ref):
    x = x_ref[...]
    sh2, sc2, g2 = (mod_ref[:, :, k * D_MODEL:(k + 1) * D_MODEL] for k in (3, 4, 5))
    hb = _flat((_rms(x, gpre_ref[...]) * (1.0 + sc2) + sh2).astype(BF16))
    for c in range(FF_HIDDEN // FFN_CHUNK):
        lo = c * FFN_CHUNK
        g = _dot(hb, w1_ref[:, lo:lo + FFN_CHUNK])
        u = _dot(hb, w1_ref[:, FF_HIDDEN + lo:FF_HIDDEN + lo + FFN_CHUNK])
        act_ref[:, lo:lo + FFN_CHUNK] = (g * _sigmoid(g) * u).astype(BF16)
    y = _dot(act_ref[...], w2_ref[...])
    y_ref[...] = x + g2 * _tiled(_rms(y, gpost_ref[...]))


def _ffn(grp, l, x, mod, g_pre, g_post, w1, w2):
    assert grp.chunk_len % FFN_STEPS == 0
    return pl.pallas_call(
        _ffn_kernel,
        grid=(grp.lru_groups, grp.chunk_len // FFN_STEPS),
        in_specs=[grp.seq_major(D_MODEL, FFN_STEPS), grp.per_pseudo(N_MOD * D_MODEL, l),
                  _resident((1, D_MODEL), l), _resident((1, D_MODEL), l), _resident(w1.shape[1:], l),
                  _resident(w2.shape[1:], l)],
        out_specs=grp.seq_major(D_MODEL, FFN_STEPS),
        out_shape=grp.shape3(D_MODEL, F32),
        scratch_shapes=[pltpu.VMEM((SUBLANES * FFN_STEPS, FF_HIDDEN), BF16)],
        compiler_params=_params(2),
        name="ffn",
    )(x, mod, g_pre, g_post, w1, w2)


def _rot_partner(w):
    half = QK_ROPE // 2
    return jnp.concatenate([-w[..., half:], w[..., :half]], axis=-1)


def _rope_tables(grp):
    n_tokens = grp.seq_len
    rows = n_tokens // GRID_W
    row = jnp.repeat(jnp.arange(rows, dtype=F32), GRID_W)
    col = jnp.tile(jnp.arange(GRID_W, dtype=F32), rows)
    n_freq = QK_ROPE // 4
    inv = ROPE_THETA ** (-jnp.arange(n_freq, dtype=F32) / n_freq)
    ang = jnp.concatenate([row[:, None] * inv, col[:, None] * inv], axis=-1)
    cos, sin = jnp.cos(ang), jnp.sin(ang)
    cos2, sin2 = jnp.concatenate([cos, cos], -1), jnp.concatenate([sin, sin], -1)
    ktab = jnp.concatenate([cos2, sin2, jnp.zeros((n_tokens, LANES - 2 * QK_ROPE), F32)], -1)
    scale = QK_DIM ** -0.5
    tail = jnp.zeros((n_tokens, HEAD_LANES - QK_DIM), F32)
    qcos = jnp.concatenate([jnp.ones((n_tokens, QK_NOPE), F32), cos2, tail], -1) * scale
    qsin = jnp.concatenate([jnp.zeros((n_tokens, QK_NOPE), F32), sin2, tail], -1) * scale

    def by_pseudo(tab):
        tab = tab.reshape(grp.n_chunks, grp.chunk_len, tab.shape[-1])
        return jnp.tile(tab, (SUBLANES // grp.n_chunks, 1, 1))

    return by_pseudo(qcos), by_pseudo(qsin), by_pseudo(ktab)


def kernel(x_prompt, x_sample, cache_ckv, cache_krope, state_lru_fwd, state_lru_bwd, c, c_ctx, w_mod, b_mod, g_pre_mix, g_post_mix, g_pre_ffn, g_post_ffn, w_in, g_q, w_q_up, g_kv, w_kv_up, w_o_mla, conv_w, conv_b, w_rg, b_rg, w_ig, b_ig, lru_lambda, w_o_lru, w_out, w_ffn_in, w_ffn_out):
    batch, seq = x_prompt.shape[:2]
    dec_batch, dec_seq = x_sample.shape[:2]
    past = cache_ckv.shape[2]

    w_in_r = _w_in_layout(w_in)
    wq = w_q_up.reshape(DEPTH, Q_LORA, N_HEADS, QK_DIM)
    q_tail = jnp.zeros((DEPTH, Q_LORA, N_HEADS, HEAD_LANES - QK_DIM), F32)
    q_main = jnp.concatenate([wq, q_tail], axis=-1)
    q_partner = jnp.concatenate([jnp.zeros_like(wq[..., :QK_NOPE]), _rot_partner(wq[..., QK_NOPE:]), q_tail], axis=-1)
    w_q_r = jnp.concatenate([q_main.reshape(DEPTH, Q_LORA, Q_COLS),
                             q_partner.reshape(DEPTH, Q_LORA, Q_COLS)], axis=-1).astype(BF16)
    w_kv_r = w_kv_up.astype(BF16)
    wo = w_o_mla.reshape(DEPTH, N_HEADS, V_DIM, D_MODEL)
    w_o_mla_b = jnp.concatenate([jnp.zeros((DEPTH, N_HEADS, HEAD_LANES - V_DIM, D_MODEL), F32), wo],
                                axis=2).reshape(DEPTH, ATTN_OUT, D_MODEL).astype(BF16)
    w_o_lru_b, w_out_b = w_o_lru.astype(BF16), w_out.astype(BF16)
    w_rg_b, w_ig_b = (0.5 * w_rg).astype(BF16), (0.5 * w_ig).astype(BF16)
    b_rg, b_ig = 0.5 * b_rg, 0.5 * b_ig
    w1_b, w2_b = w_ffn_in.astype(BF16), w_ffn_out.astype(BF16)
    row = lambda v: v.reshape(DEPTH, 1, -1)
    g_pre_mix, g_post_mix, g_pre_ffn, g_post_ffn = map(row, (g_pre_mix, g_post_mix, g_pre_ffn, g_post_ffn))
    g_q, g_kv, conv_b = row(g_q), row(g_kv), row(conv_b)
    perm = _to_time_major()
    perm_b, perm_t_b = jnp.asarray(perm, BF16), jnp.asarray(perm.T, BF16)

    mod_all = _modulation(jnp.concatenate([c_ctx[None, :], c], axis=0), w_mod, b_mod)

    def run_group(grp, x, mod_rows, layer_inputs):
        x = x.reshape(grp.n_pseudo, grp.chunk_len, D_MODEL)
        rope_tabs = _rope_tables(grp) if grp.latent else None
        flat = lambda a: a.reshape(grp.tokens, a.shape[-1])
        mod = mod_all[:, mod_rows][:, :, None, :]
        per_layer = []
        for l in range(DEPTH):
            q, ckv, kv, kr, lx, gg, gates = _in_proj(grp, l, x, mod, perm_b, g_pre_mix, w_in_r, g_q, w_q_r,
                                                     g_kv, w_kv_r, rope_tabs)
            hf0, hb0, kv_ctx, kr_ctx = layer_inputs(l)
            o = _attention(grp, flat(q), flat(kv), flat(kr), kv_ctx, kr_ctx)
            lru = _rg_lru(grp, l, lx, gg, conv_w, conv_b, w_rg_b, b_rg, w_ig_b, b_ig, lru_lambda,
                          hf0, hb0, not grp.latent)
            o = o.reshape(grp.n_pseudo, grp.chunk_len, ATTN_OUT)
            x = _merge(grp, l, o, lru[0], gates, x, mod, perm_t_b, g_post_mix, w_o_mla_b, w_o_lru_b, w_out_b)
            x = _ffn(grp, l, x, mod, g_pre_ffn, g_post_ffn, w1_b, w2_b)
            per_layer.append((ckv, kr) + tuple(lru[1:]))
        return x, per_layer

    ctx = _Group(batch, seq, latent=False)
    zeros_state = jnp.zeros((batch, LRU_WIDTH), F32)
    y_prompt, ctx_layers = run_group(ctx, x_prompt, np.zeros(ctx.n_pseudo, np.int32),
                                     lambda l: (zeros_state, zeros_state, None, None))
    stack = lambda k, shape: jnp.stack([lay[k].reshape(shape) for lay in ctx_layers], axis=1)
    new_ckv = stack(0, (batch, seq, KV_LORA))
    new_krope = stack(1, (batch, seq, QK_ROPE))
    new_lru_fwd = stack(2, (batch, LRU_WIDTH))
    new_lru_bwd = stack(3, (batch, LRU_WIDTH))

    lat = _Group(dec_batch, dec_seq, latent=True)

    def latent_inputs(l):
        per_chunk = lambda s: jnp.repeat(s[:, l], lat.n_chunks, axis=0)
        kv_ctx = _kv_up(l, cache_ckv[:, l].reshape(dec_batch * past, KV_LORA), w_kv_r, past)
        return (per_chunk(state_lru_fwd), per_chunk(state_lru_bwd), kv_ctx,
                cache_krope[:, l].reshape(dec_batch * past, QK_ROPE))

    y_sample, _ = run_group(lat, x_sample, 1 + np.arange(lat.n_pseudo) // lat.n_chunks, latent_inputs)
    return (y_prompt.reshape(batch, seq, D_MODEL), y_sample.reshape(dec_batch, dec_seq, D_MODEL),
            new_ckv, new_krope, new_lru_fwd, new_lru_bwd)
```

```python
import functools

import numpy as np

import jax
import jax.numpy as jnp
from jax import lax
from jax.experimental import pallas as pl
from jax.experimental.pallas import tpu as pltpu

D_MODEL = 1024
DEPTH = 2
GRID_W = 64
N_HEADS = 8
QK_NOPE = 64
QK_ROPE = 32
QK_DIM = QK_NOPE + QK_ROPE
V_DIM = 64
Q_LORA = 384
KV_LORA = 256
ROPE_THETA = 10000.0
LRU_WIDTH = 1024
LRU_BLOCKS = 8
LRU_BW = LRU_WIDTH // LRU_BLOCKS
CONV_W = 4
LRU_C = 8.0
FF_HIDDEN = 2816
N_MOD = 6
EPS = 1e-6

SUBLANES = 8
LANES = 128
VMEM_LIMIT_BYTES = 56 * 1024 * 1024

C_Q = 0
C_KV = C_Q + Q_LORA
C_LX = C_KV + KV_LORA
C_LG = C_LX + LRU_WIDTH
C_MG = C_LG + LRU_WIDTH
C_KR = C_MG + 2 * D_MODEL
IN_COLS_PADDED = C_KR + LANES
HEAD_LANES = LANES
assert QK_DIM <= HEAD_LANES and QK_NOPE + V_DIM == HEAD_LANES
Q_COLS = N_HEADS * HEAD_LANES
KV_COLS = N_HEADS * HEAD_LANES
ATTN_OUT = N_HEADS * HEAD_LANES

STEPS = 64
TILE_ROWS = SUBLANES * STEPS
PERM_STEPS = 32
PERM_ROWS = SUBLANES * PERM_STEPS
assert STEPS % PERM_STEPS == 0
Q_BLOCK = 1024
ATTN_SEQS = 8
FFN_STEPS = 128
FFN_CHUNK = 256
LRU_CH = 256
LRU_ROWS = 512
SCAN_BLOCK = 4
SCAN_UNROLL = 2
MOD_TILE = 1536

F32 = jnp.float32
BF16 = jnp.bfloat16


def _params(n_grid_dims):
    return pltpu.CompilerParams(dimension_semantics=("arbitrary",) * n_grid_dims,
                                vmem_limit_bytes=VMEM_LIMIT_BYTES)


def _resident(shape, layer=None):
    zeros = (0,) * len(shape)
    if layer is None:
        return pl.BlockSpec(shape, lambda *_: zeros, pipeline_mode=pl.Buffered(1))
    return pl.BlockSpec((None,) + tuple(shape), lambda *_: (layer,) + zeros, pipeline_mode=pl.Buffered(1))


def _rms(x, g):
    return x * lax.rsqrt(jnp.mean(x * x, axis=-1, keepdims=True) + EPS) * g


def _dot(a, b):
    return jnp.dot(a, b, preferred_element_type=F32)


def _dot_t(a, b):
    return lax.dot_general(a, b, (((1,), (1,)), ((), ())), preferred_element_type=F32)


def _flat(x3):
    return x3.reshape(x3.shape[0] * x3.shape[1], x3.shape[-1])


def _tiled(x2):
    return x2.reshape(SUBLANES, x2.shape[0] // SUBLANES, x2.shape[-1])


def _sigmoid(x):
    return 0.5 * jnp.tanh(0.5 * x) + 0.5


def _to_time_major():
    r = np.arange(PERM_ROWS)
    perm = np.zeros((PERM_ROWS, PERM_ROWS), np.float32)
    perm[r, (r % SUBLANES) * PERM_STEPS + r // SUBLANES] = 1.0
    return perm


def _time_major_rows(perm, x3):
    parts = [_dot(perm, _flat(x3[:, t:t + PERM_STEPS, :])).astype(BF16) for t in range(0, x3.shape[1], PERM_STEPS)]
    return jnp.concatenate(parts, axis=0)


def _seq_major_tile(perm_t, x2):
    parts = [_tiled(_dot(perm_t, x2[r:r + PERM_ROWS, :]).astype(BF16)) for r in range(0, x2.shape[0], PERM_ROWS)]
    return jnp.concatenate(parts, axis=1)


class _Group:
    def __init__(self, batch, seq_len, latent):
        self.batch, self.seq_len, self.latent = batch, seq_len, latent
        self.tokens = batch * seq_len
        self.n_chunks = max(1, SUBLANES // batch)
        self.n_pseudo = batch * self.n_chunks
        assert self.n_chunks in (1, 2) and self.n_pseudo % SUBLANES == 0
        self.chunk_len = seq_len // self.n_chunks
        assert self.chunk_len % STEPS == 0 and seq_len % min(Q_BLOCK, seq_len) == 0
        self.lru_groups = self.n_pseudo // SUBLANES
        self.tiles_per_chunk = self.chunk_len // STEPS
        self.grid = (self.lru_groups, self.tiles_per_chunk)

    def seq_major(self, width, steps=STEPS):
        return pl.BlockSpec((SUBLANES, steps, width), lambda g, i: (g, i, 0))

    def time_major(self, width):
        return pl.BlockSpec((TILE_ROWS, width), lambda g, i: (g * self.tiles_per_chunk + i, 0))

    def per_pseudo(self, width, layer):
        return pl.BlockSpec((None, SUBLANES, 1, width), lambda g, i: (layer, g, 0, 0))

    def by_position(self, width):
        return pl.BlockSpec((SUBLANES, STEPS, width), lambda g, i: (0, i, 0))

    def shape3(self, width, dtype):
        return jax.ShapeDtypeStruct((self.n_pseudo, self.chunk_len, width), dtype)

    def shape_tm(self, width, dtype):
        return jax.ShapeDtypeStruct((self.lru_groups * self.chunk_len * SUBLANES, width), dtype)


def _mod_kernel(c_ref, w_ref, b_ref, o_ref):
    c = c_ref[...]
    a = (c * _sigmoid(c)).astype(BF16)
    o_ref[...] = _dot(a, w_ref[...].astype(BF16)) + b_ref[...]


def _modulation(cond, w_mod, b_mod):
    n = N_MOD * D_MODEL
    rows = cond.shape[0]
    return pl.pallas_call(
        _mod_kernel,
        grid=(DEPTH, n // MOD_TILE),
        in_specs=[pl.BlockSpec((rows, D_MODEL), lambda l, j: (0, 0)),
                  pl.BlockSpec((None, D_MODEL, MOD_TILE), lambda l, j: (l, 0, j)),
                  pl.BlockSpec((None, 1, MOD_TILE), lambda l, j: (l, 0, j))],
        out_specs=pl.BlockSpec((None, rows, MOD_TILE), lambda l, j: (l, 0, j)),
        out_shape=jax.ShapeDtypeStruct((DEPTH, rows, n), F32),
        compiler_params=_params(2),
        name="modulation",
    )(cond, w_mod, b_mod.reshape(DEPTH, 1, n))


def _w_in_layout_kernel(w_ref, tail_ref, o_ref):
    off_kr = Q_LORA + KV_LORA
    lane = lax.broadcasted_iota(jnp.int32, (1, LANES), 1)
    half = QK_ROPE // 2
    whole = w_ref.shape[-1]
    o_ref[:, C_Q:C_LX] = w_ref[:, :off_kr].astype(BF16)
    o_ref[:, C_LX:whole - QK_ROPE] = w_ref[:, off_kr + QK_ROPE:].astype(BF16)
    o_ref[:, whole - QK_ROPE:C_KR] = tail_ref[:, :C_KR - (whole - QK_ROPE)].astype(BF16)
    tile = w_ref[:, off_kr:off_kr + LANES]
    partner = jnp.where(lane < half, -pltpu.roll(tile, LANES - half, 1), pltpu.roll(tile, half, 1))
    seg = jnp.where(lane < QK_ROPE, tile, jnp.where(lane < 2 * QK_ROPE, pltpu.roll(partner, QK_ROPE, 1), 0.0))
    o_ref[:, C_KR:IN_COLS_PADDED] = seg.astype(BF16)


def _w_in_layout(w_in):
    rows = 256
    whole_tiles = w_in.shape[-1] // LANES
    return pl.pallas_call(
        _w_in_layout_kernel,
        grid=(DEPTH, D_MODEL // rows),
        in_specs=[pl.BlockSpec((None, rows, whole_tiles * LANES), lambda l, i: (l, i, 0)),
                  pl.BlockSpec((None, rows, LANES), lambda l, i: (l, i, whole_tiles))],
        out_specs=pl.BlockSpec((None, rows, IN_COLS_PADDED), lambda l, i: (l, i, 0)),
        out_shape=jax.ShapeDtypeStruct((DEPTH, D_MODEL, IN_COLS_PADDED), BF16),
        compiler_params=_params(2),
        name="w_in_layout",
    )(w_in, w_in)


def _in_proj_kernel(latent, x_ref, mod_ref, perm_ref, gpre_ref, win_ref, gq_ref, wq_ref, gkv_ref, wkv_ref, *rest):
    if latent:
        cosq_ref, sinq_ref, ktab_ref = rest[:3]
        rest = rest[3:]
    q_ref, ckv_ref, kv_ref, kr_ref, lx_ref, gg_ref, gates_ref = rest
    sh1, sc1 = mod_ref[:, :, 0:D_MODEL], mod_ref[:, :, D_MODEL:2 * D_MODEL]
    hb3 = (_rms(x_ref[...], gpre_ref[...]) * (1.0 + sc1) + sh1).astype(BF16)
    hb = _flat(hb3)

    def proj(lhs, lo, hi):
        return _dot(lhs, win_ref[:, lo:hi])

    qn = _rms(proj(hb, C_Q, C_KV), gq_ref[...]).astype(BF16)
    qf = _dot(qn, wq_ref[...])
    if latent:
        cos, sin = _flat(cosq_ref[...]), _flat(sinq_ref[...])
        for h in range(N_HEADS):
            lanes = slice(h * HEAD_LANES, (h + 1) * HEAD_LANES)
            partner = slice(Q_COLS + h * HEAD_LANES, Q_COLS + (h + 1) * HEAD_LANES)
            q_ref[:, :, lanes] = _tiled((qf[:, lanes] * cos + qf[:, partner] * sin).astype(BF16))
    else:
        q_ref[...] = _tiled((qf * QK_DIM ** -0.5).astype(BF16))

    ckv = _rms(proj(hb, C_KV, C_LX), gkv_ref[...])
    ckv_ref[...] = _tiled(ckv)
    kv_ref[...] = _tiled(_dot(ckv.astype(BF16), wkv_ref[...]).astype(BF16))

    kr = proj(hb, C_KR, IN_COLS_PADDED)
    if latent:
        pr = kr * _flat(ktab_ref[...])
        kr = pr + pltpu.roll(pr, LANES - QK_ROPE, 1)
    kr_ref[...] = _tiled(kr[:, :QK_ROPE])

    gates_ref[...] = _tiled(_sigmoid(proj(hb, C_MG, C_KR)))

    hb_tm = _time_major_rows(perm_ref[...], hb3)
    lx_ref[...] = proj(hb_tm, C_LX, C_LG)
    gg_ref[...] = jax.nn.gelu(proj(hb_tm, C_LG, C_MG))


def _in_proj(grp, l, x, mod, perm, g_pre, w_in, g_q, w_q, g_kv, w_kv, rope_tabs):
    q_cols = w_q.shape[-1] if grp.latent else Q_COLS
    in_specs = [grp.seq_major(D_MODEL), grp.per_pseudo(N_MOD * D_MODEL, l), _resident(perm.shape),
                _resident((1, D_MODEL), l), _resident(w_in.shape[1:], l), _resident((1, Q_LORA), l),
                _resident((Q_LORA, q_cols), l), _resident((1, KV_LORA), l), _resident(w_kv.shape[1:], l)]
    args = [x, mod, perm, g_pre, w_in, g_q, w_q, g_kv, w_kv]
    if grp.latent:
        in_specs += [grp.by_position(HEAD_LANES), grp.by_position(HEAD_LANES), grp.by_position(LANES)]
        args += list(rope_tabs)
    return pl.pallas_call(
        functools.partial(_in_proj_kernel, grp.latent),
        grid=grp.grid,
        in_specs=in_specs,
        out_specs=[grp.seq_major(Q_COLS), grp.seq_major(KV_LORA), grp.seq_major(KV_COLS), grp.seq_major(QK_ROPE),
                   grp.time_major(LRU_WIDTH), grp.time_major(LRU_WIDTH), grp.seq_major(2 * D_MODEL)],
        out_shape=[grp.shape3(Q_COLS, BF16), grp.shape3(KV_LORA, F32), grp.shape3(KV_COLS, BF16),
                   grp.shape3(QK_ROPE, F32), grp.shape_tm(LRU_WIDTH, F32), grp.shape_tm(LRU_WIDTH, F32),
                   grp.shape3(2 * D_MODEL, F32)],
        compiler_params=_params(2),
        name="in_proj",
    )(*args)


def _kv_up_kernel(c_ref, w_ref, o_ref):
    o_ref[...] = _dot(c_ref[...].astype(BF16), w_ref[...]).astype(BF16)


def _kv_up(l, ckv, w_kv, rows_per_step):
    n = ckv.shape[0]
    return pl.pallas_call(
        _kv_up_kernel,
        grid=(n // rows_per_step,),
        in_specs=[pl.BlockSpec((rows_per_step, KV_LORA), lambda i: (i, 0)), _resident(w_kv.shape[1:], l)],
        out_specs=pl.BlockSpec((rows_per_step, KV_COLS), lambda i: (i, 0)),
        out_shape=jax.ShapeDtypeStruct((n, KV_COLS), BF16),
        compiler_params=_params(1),
        name="kv_up_ctx",
    )(ckv, w_kv)


def _attn_kernel(with_ctx, n_seq, q_ref, kv_ref, kr_ref, place_ref, *rest):
    if with_ctx:
        kvc_ref, krc_ref, o_ref = rest
    else:
        (o_ref,) = rest
    is_nope = lax.broadcasted_iota(jnp.int32, (1, HEAD_LANES), 1) < QK_NOPE

    def keys_of(kv, kr, rows):
        kr_placed = _dot(kr[rows, :].astype(BF16), place_ref[...]).astype(BF16)
        return lambda h: jnp.where(is_nope, kv[rows, h * HEAD_LANES:(h + 1) * HEAD_LANES], kr_placed)

    share = lambda ref, j: slice(j * (ref.shape[0] // n_seq), (j + 1) * (ref.shape[0] // n_seq))
    for j in range(n_seq):
        q_rows, k_rows = share(q_ref, j), share(kv_ref, j)
        keys = keys_of(kv_ref, kr_ref, k_rows)
        if with_ctx:
            c_rows = share(kvc_ref, j)
            keys_ctx = keys_of(kvc_ref, krc_ref, c_rows)
        for h in range(N_HEADS):
            lanes = slice(h * HEAD_LANES, (h + 1) * HEAD_LANES)
            q = q_ref[q_rows, lanes]
            s = _dot_t(q, keys(h))
            m = jnp.max(s, axis=-1, keepdims=True)
            if with_ctx:
                sc = _dot_t(q, keys_ctx(h))
                m = jnp.maximum(m, jnp.max(sc, axis=-1, keepdims=True))
            p = jnp.exp(s - m)
            den = jnp.sum(p, axis=-1, keepdims=True)
            o = _dot(p.astype(BF16), kv_ref[k_rows, lanes])
            if with_ctx:
                pc = jnp.exp(sc - m)
                den = den + jnp.sum(pc, axis=-1, keepdims=True)
                o = o + _dot(pc.astype(BF16), kvc_ref[c_rows, lanes])
            o_ref[q_rows, lanes] = (o / den).astype(BF16)


def _attention(grp, q, kv, kr, kv_ctx=None, kr_ctx=None):
    with_ctx = kv_ctx is not None
    q_block = min(Q_BLOCK, grp.seq_len)
    nq = grp.seq_len // q_block
    n_seq = ATTN_SEQS if nq == 1 and grp.batch % ATTN_SEQS == 0 else 1
    seq = lambda rows, width: pl.BlockSpec((n_seq * rows, width), lambda b, i: (b, 0))
    place = np.zeros((QK_ROPE, HEAD_LANES), np.float32)
    place[np.arange(QK_ROPE), QK_NOPE + np.arange(QK_ROPE)] = 1.0
    in_specs = [pl.BlockSpec((n_seq * q_block, Q_COLS), lambda b, i: (b * nq + i, 0)),
                seq(grp.seq_len, KV_COLS), seq(grp.seq_len, QK_ROPE), _resident(place.shape)]
    args = [q, kv, kr, jnp.asarray(place, BF16)]
    if with_ctx:
        past = kv_ctx.shape[0] // grp.batch
        in_specs += [seq(past, KV_COLS), seq(past, QK_ROPE)]
        args += [kv_ctx, kr_ctx]
    return pl.pallas_call(
        functools.partial(_attn_kernel, with_ctx, n_seq),
        grid=(grp.batch // n_seq, nq),
        in_specs=in_specs,
        out_specs=pl.BlockSpec((n_seq * q_block, ATTN_OUT), lambda b, i: (b * nq + i, 0)),
        out_shape=jax.ShapeDtypeStruct((grp.tokens, ATTN_OUT), BF16),
        compiler_params=_params(2),
        name="attention",
    )(*args)


def _softplus(z):
    return jnp.maximum(z, 0.0) + jnp.log1p(jnp.exp(-jnp.abs(z)))


def _lru_kernel(n_chunks, chunk_len, with_state, x_ref, gg_ref, cw_ref, cb_ref, wrg_ref, brg_ref,
                wig_ref, big_ref, lam_ref, hf0_ref, hb0_ref, *rest):
    if with_state:
        s_ref, hf_ref, hb_ref, xpad, a_f, u_f, a_b, u_b = rest
    else:
        s_ref, xpad, a_f, u_f, a_b, u_b = rest
    rows = chunk_len * SUBLANES
    halo = (CONV_W // 2) * SUBLANES
    sub = lax.broadcasted_iota(jnp.int32, (SUBLANES, LRU_CH), 0)
    first_chunk = (sub % n_chunks) == 0
    last_chunk = (sub % n_chunks) == n_chunks - 1

    def from_prev(v):
        return jnp.where(first_chunk, 0.0, pltpu.roll(v, 1, 0))

    def from_next(v):
        return jnp.where(last_chunk, 0.0, pltpu.roll(v, SUBLANES - 1, 0))

    xpad[halo:halo + rows, :] = x_ref[...]
    xpad[0:SUBLANES, :] = from_prev(x_ref[rows - 2 * SUBLANES:rows - SUBLANES, :])
    xpad[SUBLANES:halo, :] = from_prev(x_ref[rows - SUBLANES:rows, :])
    xpad[halo + rows:halo + rows + SUBLANES, :] = from_next(x_ref[0:SUBLANES, :])

    def gates(c, carry):
        r0 = pl.multiple_of(c * LRU_ROWS, LRU_ROWS)
        xc = cb_ref[...]
        for k in range(CONV_W):
            xc = xc + xpad[pl.ds(r0 + k * SUBLANES, LRU_ROWS), :] * cw_ref[k:k + 1, :]
        xcb = xc.astype(BF16)
        for d, (a_ref, u_ref) in enumerate(((a_f, u_f), (a_b, u_b))):
            for n in range(LRU_CH // LRU_BW):
                sl = slice(n * LRU_BW, (n + 1) * LRU_BW)
                t_r = jnp.tanh(_dot(xcb[:, sl], wrg_ref[d, n]) + brg_ref[d:d + 1, sl])
                t_i = jnp.tanh(_dot(xcb[:, sl], wig_ref[d, n]) + big_ref[d:d + 1, sl])
                half_c = (-0.5 * LRU_C) * _softplus(-lam_ref[d:d + 1, sl])
                log_a = half_c * t_r + half_c
                a = jnp.exp(log_a)
                a_ref[pl.ds(r0, LRU_ROWS), sl] = a
                one_minus_a2 = -jnp.tanh(log_a) * (a * a + 1.0)
                root = jnp.where(one_minus_a2 > 0.0, one_minus_a2 * lax.rsqrt(one_minus_a2), 0.0)
                u_ref[pl.ds(r0, LRU_ROWS), sl] = root * (0.5 * t_i + 0.5) * xc[:, sl]
        return carry

    lax.fori_loop(0, rows // LRU_ROWS, gates, 0)

    block_rows = SCAN_BLOCK * SUBLANES

    def advance(a_ref, u_ref, first_step, order, h, store):
        slab = pl.ds(pl.multiple_of(first_step * SUBLANES, block_rows), block_rows)
        a_all, u_all = a_ref[slab, :], u_ref[slab, :]
        a = [a_all[j * SUBLANES:(j + 1) * SUBLANES] for j in order]
        u = [u_all[j * SUBLANES:(j + 1) * SUBLANES] for j in order]
        a01, u01 = a[1] * a[0], a[1] * u[0] + u[1]
        a23, u23 = a[3] * a[2], a[3] * u[2] + u[3]
        a03, u03 = a23 * a01, a23 * u01 + u23
        h4 = a03 * h + u03
        if store:
            h2 = a01 * h + u01
            states = dict(zip(order, (a[0] * h + u[0], h2, a[2] * h2 + u[2], h4)))
            u_ref[slab, :] = jnp.concatenate([states[j] for j in range(SCAN_BLOCK)], axis=0)
        return h4

    def scan(hf, hb, store):
        ascending, descending = tuple(range(SCAN_BLOCK)), tuple(reversed(range(SCAN_BLOCK)))

        def block(k, carry):
            hf, hb = carry
            hf = advance(a_f, u_f, k * SCAN_BLOCK, ascending, hf, store)
            hb = advance(a_b, u_b, chunk_len - (k + 1) * SCAN_BLOCK, descending, hb, store)
            return hf, hb
        return lax.fori_loop(0, chunk_len // SCAN_BLOCK, block, (hf, hb), unroll=SCAN_UNROLL)

    hf, hb = hf0_ref[...], hb0_ref[...]
    if n_chunks > 1:
        ef, eb = scan(hf, hb, False)
        hf = jnp.where(first_chunk, hf, pltpu.roll(ef, 1, 0))
        hb = jnp.where(last_chunk, hb, pltpu.roll(eb, SUBLANES - 1, 0))
    hf, hb = scan(hf, hb, True)
    if with_state:
        hf_ref[...] = hf
        hb_ref[...] = hb

    def combine(c, carry):
        r = pl.ds(pl.multiple_of(c * LRU_ROWS, LRU_ROWS), LRU_ROWS)
        s_ref[r, :] = ((u_f[r, :] + u_b[r, :]) * gg_ref[r, :]).astype(BF16)
        return carry

    lax.fori_loop(0, rows // LRU_ROWS, combine, 0)


def _rg_lru(grp, l, lx, gg, conv_w, conv_b, w_rg, b_rg, w_ig, b_ig, lam, hf0, hb0, with_state):
    assert grp.n_chunks == 1 or not with_state
    rows = grp.chunk_len * SUBLANES
    nb = LRU_CH // LRU_BW
    tile = pl.BlockSpec((rows, LRU_CH), lambda g, c: (g, c))
    chan = lambda r: pl.BlockSpec((None, r, LRU_CH), lambda g, c: (l, 0, c))
    wblk = pl.BlockSpec((None, 2, nb, LRU_BW, LRU_BW), lambda g, c: (l, 0, c, 0, 0))
    state = pl.BlockSpec((SUBLANES, LRU_CH), lambda g, c: (g, c))
    out_specs = [tile]
    out_shape = [grp.shape_tm(LRU_WIDTH, BF16)]
    if with_state:
        out_specs += [state, state]
        out_shape += [jax.ShapeDtypeStruct((grp.n_pseudo, LRU_WIDTH), F32)] * 2
    scratch = [pltpu.VMEM((rows + (CONV_W - 1) * SUBLANES, LRU_CH), F32)] + [pltpu.VMEM((rows, LRU_CH), F32)] * 4
    return pl.pallas_call(
        functools.partial(_lru_kernel, grp.n_chunks, grp.chunk_len, with_state),
        grid=(grp.lru_groups, LRU_WIDTH // LRU_CH),
        in_specs=[tile, tile, chan(CONV_W), chan(1), wblk, chan(2), wblk, chan(2), chan(2), state, state],
        out_specs=out_specs,
        out_shape=out_shape,
        scratch_shapes=scratch,
        compiler_params=_params(2),
        name="rg_lru",
    )(lx, gg, conv_w, conv_b, w_rg, b_rg, w_ig, b_ig, lam, hf0, hb0)


def _merge_kernel(o_ref, s_ref, gates_ref, x_ref, mod_ref, perm_ref, gpost_ref, wom_ref, wol_ref, wout_ref, y_ref):
    y_mla = _dot(_flat(o_ref[...]), wom_ref[...])
    s = _flat(_seq_major_tile(perm_ref[...], s_ref[...]))
    y_lru = _dot(s, wol_ref[...])
    gates = _flat(gates_ref[...])
    z = gates[:, :D_MODEL] * y_mla + gates[:, D_MODEL:] * y_lru
    y = _dot(z.astype(BF16), wout_ref[...])
    g1 = mod_ref[:, :, 2 * D_MODEL:3 * D_MODEL]
    y_ref[...] = x_ref[...] + g1 * _tiled(_rms(y, gpost_ref[...]))


def _merge(grp, l, o, s, gates, x, mod, perm_t, g_post, w_o_mla, w_o_lru, w_out):
    return pl.pallas_call(
        _merge_kernel,
        grid=grp.grid,
        in_specs=[grp.seq_major(ATTN_OUT), grp.time_major(LRU_WIDTH), grp.seq_major(2 * D_MODEL),
                  grp.seq_major(D_MODEL), grp.per_pseudo(N_MOD * D_MODEL, l), _resident(perm_t.shape),
                  _resident((1, D_MODEL), l), _resident(w_o_mla.shape[1:], l), _resident(w_o_lru.shape[1:], l),
                  _resident(w_out.shape[1:], l)],
        out_specs=grp.seq_major(D_MODEL),
        out_shape=grp.shape3(D_MODEL, F32),
        compiler_params=_params(2),
        name="merge",
    )(o, s, gates, x, mod, perm_t, g_post, w_o_mla, w_o_lru, w_out)


def _ffn_kernel(x_ref, mod_ref, gpre_ref, gpost_ref, w1_ref, w2_ref, y_ref, act_ref):
    x = x_ref[...]
    sh2, sc2, g2 = (mod_ref[:, :, k * D_MODEL:(k + 1) * D_MODEL] for k in (3, 4, 5))
    hb = _flat((_rms(x, gpre_ref[...]) * (1.0 + sc2) + sh2).astype(BF16))
    for c in range(FF_HIDDEN // FFN_CHUNK):
        lo = c * FFN_CHUNK
        g = _dot(hb, w1_ref[:, lo:lo + FFN_CHUNK])
        u = _dot(hb, w1_ref[:, FF_HIDDEN + lo:FF_HIDDEN + lo + FFN_CHUNK])
        act_ref[:, lo:lo + FFN_CHUNK] = (g * _sigmoid(g) * u).astype(BF16)
    y = _dot(act_ref[...], w2_ref[...])
    y_ref[...] = x + g2 * _tiled(_rms(y, gpost_ref[...]))


def _ffn(grp, l, x, mod, g_pre, g_post, w1, w2):
    assert grp.chunk_len % FFN_STEPS == 0
    return pl.pallas_call(
        _ffn_kernel,
        grid=(grp.lru_groups, grp.chunk_len // FFN_STEPS),
        in_specs=[grp.seq_major(D_MODEL, FFN_STEPS), grp.per_pseudo(N_MOD * D_MODEL, l),
                  _resident((1, D_MODEL), l), _resident((1, D_MODEL), l), _resident(w1.shape[1:], l),
                  _resident(w2.shape[1:], l)],
        out_specs=grp.seq_major(D_MODEL, FFN_STEPS),
        out_shape=grp.shape3(D_MODEL, F32),
        scratch_shapes=[pltpu.VMEM((SUBLANES * FFN_STEPS, FF_HIDDEN), BF16)],
        compiler_params=_params(2),
        name="ffn",
    )(x, mod, g_pre, g_post, w1, w2)


def _rot_partner(w):
    half = QK_ROPE // 2
    return jnp.concatenate([-w[..., half:], w[..., :half]], axis=-1)


def _rope_tables(grp):
    n_tokens = grp.seq_len
    rows = n_tokens // GRID_W
    row = jnp.repeat(jnp.arange(rows, dtype=F32), GRID_W)
    col = jnp.tile(jnp.arange(GRID_W, dtype=F32), rows)
    n_freq = QK_ROPE // 4
    inv = ROPE_THETA ** (-jnp.arange(n_freq, dtype=F32) / n_freq)
    ang = jnp.concatenate([row[:, None] * inv, col[:, None] * inv], axis=-1)
    cos, sin = jnp.cos(ang), jnp.sin(ang)
    cos2, sin2 = jnp.concatenate([cos, cos], -1), jnp.concatenate([sin, sin], -1)
    ktab = jnp.concatenate([cos2, sin2, jnp.zeros((n_tokens, LANES - 2 * QK_ROPE), F32)], -1)
    scale = QK_DIM ** -0.5
    tail = jnp.zeros((n_tokens, HEAD_LANES - QK_DIM), F32)
    qcos = jnp.concatenate([jnp.ones((n_tokens, QK_NOPE), F32), cos2, tail], -1) * scale
    qsin = jnp.concatenate([jnp.zeros((n_tokens, QK_NOPE), F32), sin2, tail], -1) * scale

    def by_pseudo(tab):
        tab = tab.reshape(grp.n_chunks, grp.chunk_len, tab.shape[-1])
        return jnp.tile(tab, (SUBLANES // grp.n_chunks, 1, 1))

    return by_pseudo(qcos), by_pseudo(qsin), by_pseudo(ktab)


def kernel(x_prompt, x_sample, cache_ckv, cache_krope, state_lru_fwd, state_lru_bwd, c, c_ctx, w_mod, b_mod, g_pre_mix, g_post_mix, g_pre_ffn, g_post_ffn, w_in, g_q, w_q_up, g_kv, w_kv_up, w_o_mla, conv_w, conv_b, w_rg, b_rg, w_ig, b_ig, lru_lambda, w_o_lru, w_out, w_ffn_in, w_ffn_out):
    batch, seq = x_prompt.shape[:2]
    dec_batch, dec_seq = x_sample.shape[:2]
    past = cache_ckv.shape[2]

    w_in_r = _w_in_layout(w_in)
    wq = w_q_up.reshape(DEPTH, Q_LORA, N_HEADS, QK_DIM)
    q_tail = jnp.zeros((DEPTH, Q_LORA, N_HEADS, HEAD_LANES - QK_DIM), F32)
    q_main = jnp.concatenate([wq, q_tail], axis=-1)
    q_partner = jnp.concatenate([jnp.zeros_like(wq[..., :QK_NOPE]), _rot_partner(wq[..., QK_NOPE:]), q_tail], axis=-1)
    w_q_r = jnp.concatenate([q_main.reshape(DEPTH, Q_LORA, Q_COLS),
                             q_partner.reshape(DEPTH, Q_LORA, Q_COLS)], axis=-1).astype(BF16)
    w_kv_r = w_kv_up.astype(BF16)
    wo = w_o_mla.reshape(DEPTH, N_HEADS, V_DIM, D_MODEL)
    w_o_mla_b = jnp.concatenate([jnp.zeros((DEPTH, N_HEADS, HEAD_LANES - V_DIM, D_MODEL), F32), wo],
                                axis=2).reshape(DEPTH, ATTN_OUT, D_MODEL).astype(BF16)
    w_o_lru_b, w_out_b = w_o_lru.astype(BF16), w_out.astype(BF16)
    w_rg_b, w_ig_b = (0.5 * w_rg).astype(BF16), (0.5 * w_ig).astype(BF16)
    b_rg, b_ig = 0.5 * b_rg, 0.5 * b_ig
    w1_b, w2_b = w_ffn_in.astype(BF16), w_ffn_out.astype(BF16)
    row = lambda v: v.reshape(DEPTH, 1, -1)
    g_pre_mix, g_post_mix, g_pre_ffn, g_post_ffn = map(row, (g_pre_mix, g_post_mix, g_pre_ffn, g_post_ffn))
    g_q, g_kv, conv_b = row(g_q), row(g_kv), row(conv_b)
    perm = _to_time_major()
    perm_b, perm_t_b = jnp.asarray(perm, BF16), jnp.asarray(perm.T, BF16)

    mod_all = _modulation(jnp.concatenate([c_ctx[None, :], c], axis=0), w_mod, b_mod)

    def run_group(grp, x, mod_rows, layer_inputs):
        x = x.reshape(grp.n_pseudo, grp.chunk_len, D_MODEL)
        rope_tabs = _rope_tables(grp) if grp.latent else None
        flat = lambda a: a.reshape(grp.tokens, a.shape[-1])
        mod = mod_all[:, mod_rows][:, :, None, :]
        per_layer = []
        for l in range(DEPTH):
            q, ckv, kv, kr, lx, gg, gates = _in_proj(grp, l, x, mod, perm_b, g_pre_mix, w_in_r, g_q, w_q_r,
                                                     g_kv, w_kv_r, rope_tabs)
            hf0, hb0, kv_ctx, kr_ctx = layer_inputs(l)
            o = _attention(grp, flat(q), flat(kv), flat(kr), kv_ctx, kr_ctx)
            lru = _rg_lru(grp, l, lx, gg, conv_w, conv_b, w_rg_b, b_rg, w_ig_b, b_ig, lru_lambda,
                          hf0, hb0, not grp.latent)
            o = o.reshape(grp.n_pseudo, grp.chunk_len, ATTN_OUT)
            x = _merge(grp, l, o, lru[0], gates, x, mod, perm_t_b, g_post_mix, w_o_mla_b, w_o_lru_b, w_out_b)
            x = _ffn(grp, l, x, mod, g_pre_ffn, g_post_ffn, w1_b, w2_b)
            per_layer.append((ckv, kr) + tuple(lru[1:]))
        return x, per_layer

    ctx = _Group(batch, seq, latent=False)
    zeros_state = jnp.zeros((batch, LRU_WIDTH), F32)
    y_prompt, ctx_layers = run_group(ctx, x_prompt, np.zeros(ctx.n_pseudo, np.int32),
                                     lambda l: (zeros_state, zeros_state, None, None))
    stack = lambda k, shape: jnp.stack([lay[k].reshape(shape) for lay in ctx_layers], axis=1)
    new_ckv = stack(0, (batch, seq, KV_LORA))
    new_krope = stack(1, (batch, seq, QK_ROPE))
    new_lru_fwd = stack(2, (batch, LRU_WIDTH))
    new_lru_bwd = stack(3, (batch, LRU_WIDTH))

    lat = _Group(dec_batch, dec_seq, latent=True)

    def latent_inputs(l):
        per_chunk = lambda s: jnp.repeat(s[:, l], lat.n_chunks, axis=0)
        kv_ctx = _kv_up(l, cache_ckv[:, l].reshape(dec_batch * past, KV_LORA), w_kv_r, past)
        return (per_chunk(state_lru_fwd), per_chunk(state_lru_bwd), kv_ctx,
                cache_krope[:, l].reshape(dec_batch * past, QK_ROPE))

    y_sample, _ = run_group(lat, x_sample, 1 + np.arange(lat.n_pseudo) // lat.n_chunks, latent_inputs)
    return (y_prompt.reshape(batch, seq, D_MODEL), y_sample.reshape(dec_batch, dec_seq, D_MODEL),
            new_ckv, new_krope, new_lru_fwd, new_lru_bwd)
```

```python
import functools

import numpy as np

import jax
import jax.numpy as jnp
from jax import lax
from jax.experimental import pallas as pl
from jax.experimental.pallas import tpu as pltpu

D_MODEL = 1024
DEPTH = 2
GRID_W = 64
N_HEADS = 8
QK_NOPE = 64
QK_ROPE = 32
QK_DIM = QK_NOPE + QK_ROPE
V_DIM = 64
Q_LORA = 384
KV_LORA = 256
ROPE_THETA = 10000.0
LRU_WIDTH = 1024
LRU_BLOCKS = 8
LRU_BW = LRU_WIDTH // LRU_BLOCKS
CONV_W = 4
LRU_C = 8.0
FF_HIDDEN = 2816
N_MOD = 6
EPS = 1e-6

SUBLANES = 8
LANES = 128
VMEM_LIMIT_BYTES = 56 * 1024 * 1024
MERGE_FFN_VMEM_BYTES = 60 * 1024 * 1024

C_Q = 0
C_KV = C_Q + Q_LORA
C_LX = C_KV + KV_LORA
C_LG = C_LX + LRU_WIDTH
C_MG = C_LG + LRU_WIDTH
C_KR = C_MG + 2 * D_MODEL
IN_COLS_PADDED = C_KR + LANES
HEAD_LANES = LANES
assert QK_DIM <= HEAD_LANES and QK_NOPE + V_DIM == HEAD_LANES
Q_COLS = N_HEADS * HEAD_LANES
KV_COLS = N_HEADS * HEAD_LANES
ATTN_OUT = N_HEADS * HEAD_LANES

STEPS = 64
TILE_ROWS = SUBLANES * STEPS
PERM_STEPS = 32
PERM_ROWS = SUBLANES * PERM_STEPS
assert STEPS % PERM_STEPS == 0
Q_BLOCK = 1024
ATTN_SEQS = 8
FFN_STEPS = 128
FFN_CHUNK = 256
LRU_CH = 256
LRU_ROWS = 512
SCAN_BLOCK = 4
SCAN_UNROLL = 2
MOD_TILE = 1536

F32 = jnp.float32
BF16 = jnp.bfloat16


def _params(n_grid_dims):
    return pltpu.CompilerParams(dimension_semantics=("arbitrary",) * n_grid_dims,
                                vmem_limit_bytes=VMEM_LIMIT_BYTES)


def _resident(shape, layer=None):
    zeros = (0,) * len(shape)
    if layer is None:
        return pl.BlockSpec(shape, lambda *_: zeros, pipeline_mode=pl.Buffered(1))
    return pl.BlockSpec((None,) + tuple(shape), lambda *_: (layer,) + zeros, pipeline_mode=pl.Buffered(1))


def _rms(x, g):
    return x * lax.rsqrt(jnp.mean(x * x, axis=-1, keepdims=True) + EPS) * g


def _dot(a, b):
    return jnp.dot(a, b, preferred_element_type=F32)


def _dot_t(a, b):
    return lax.dot_general(a, b, (((1,), (1,)), ((), ())), preferred_element_type=F32)


def _flat(x3):
    return x3.reshape(x3.shape[0] * x3.shape[1], x3.shape[-1])


def _tiled(x2):
    return x2.reshape(SUBLANES, x2.shape[0] // SUBLANES, x2.shape[-1])


def _sigmoid(x):
    return 0.5 * jnp.tanh(0.5 * x) + 0.5


def _to_time_major():
    r = np.arange(PERM_ROWS)
    perm = np.zeros((PERM_ROWS, PERM_ROWS), np.float32)
    perm[r, (r % SUBLANES) * PERM_STEPS + r // SUBLANES] = 1.0
    return perm


def _time_major_rows(perm, x3):
    parts = [_dot(perm, _flat(x3[:, t:t + PERM_STEPS, :])).astype(BF16) for t in range(0, x3.shape[1], PERM_STEPS)]
    return jnp.concatenate(parts, axis=0)


def _seq_major_tile(perm_t, x2):
    parts = [_tiled(_dot(perm_t, x2[r:r + PERM_ROWS, :]).astype(BF16)) for r in range(0, x2.shape[0], PERM_ROWS)]
    return jnp.concatenate(parts, axis=1)


class _Group:
    def __init__(self, batch, seq_len, latent):
        self.batch, self.seq_len, self.latent = batch, seq_len, latent
        self.tokens = batch * seq_len
        self.n_chunks = max(1, SUBLANES // batch)
        self.n_pseudo = batch * self.n_chunks
        assert self.n_chunks in (1, 2) and self.n_pseudo % SUBLANES == 0
        self.chunk_len = seq_len // self.n_chunks
        assert self.chunk_len % STEPS == 0 and seq_len % min(Q_BLOCK, seq_len) == 0
        self.lru_groups = self.n_pseudo // SUBLANES
        self.tiles_per_chunk = self.chunk_len // STEPS
        self.grid = (self.lru_groups, self.tiles_per_chunk)

    def seq_major(self, width, steps=STEPS):
        return pl.BlockSpec((SUBLANES, steps, width), lambda g, i: (g, i, 0))

    def time_major(self, width):
        return pl.BlockSpec((TILE_ROWS, width), lambda g, i: (g * self.tiles_per_chunk + i, 0))

    def per_pseudo(self, width, layer):
        return pl.BlockSpec((None, SUBLANES, 1, width), lambda g, i: (layer, g, 0, 0))

    def by_position(self, width):
        return pl.BlockSpec((SUBLANES, STEPS, width), lambda g, i: (0, i, 0))

    def shape3(self, width, dtype):
        return jax.ShapeDtypeStruct((self.n_pseudo, self.chunk_len, width), dtype)

    def shape_tm(self, width, dtype):
        return jax.ShapeDtypeStruct((self.lru_groups * self.chunk_len * SUBLANES, width), dtype)


def _mod_kernel(c_ref, w_ref, b_ref, o_ref):
    c = c_ref[...]
    a = (c * _sigmoid(c)).astype(BF16)
    o_ref[...] = _dot(a, w_ref[...].astype(BF16)) + b_ref[...]


def _modulation(cond, w_mod, b_mod):
    n = N_MOD * D_MODEL
    rows = cond.shape[0]
    return pl.pallas_call(
        _mod_kernel,
        grid=(DEPTH, n // MOD_TILE),
        in_specs=[pl.BlockSpec((rows, D_MODEL), lambda l, j: (0, 0)),
                  pl.BlockSpec((None, D_MODEL, MOD_TILE), lambda l, j: (l, 0, j)),
                  pl.BlockSpec((None, 1, MOD_TILE), lambda l, j: (l, 0, j))],
        out_specs=pl.BlockSpec((None, rows, MOD_TILE), lambda l, j: (l, 0, j)),
        out_shape=jax.ShapeDtypeStruct((DEPTH, rows, n), F32),
        compiler_params=_params(2),
        name="modulation",
    )(cond, w_mod, b_mod.reshape(DEPTH, 1, n))


def _w_in_layout_kernel(w_ref, o_ref):
    off_kr = Q_LORA + KV_LORA
    lane = lax.broadcasted_iota(jnp.int32, (1, LANES), 1)
    half = QK_ROPE // 2
    o_ref[:, C_Q:C_LX] = w_ref[:, :off_kr].astype(BF16)
    o_ref[:, C_LX:C_KR] = w_ref[:, off_kr + QK_ROPE:].astype(BF16)
    tile = w_ref[:, off_kr:off_kr + LANES]
    partner = jnp.where(lane < half, -pltpu.roll(tile, LANES - half, 1), pltpu.roll(tile, half, 1))
    seg = jnp.where(lane < QK_ROPE, tile, jnp.where(lane < 2 * QK_ROPE, pltpu.roll(partner, QK_ROPE, 1), 0.0))
    o_ref[:, C_KR:IN_COLS_PADDED] = seg.astype(BF16)


def _w_in_layout(w_in):
    rows = 256
    in_cols = w_in.shape[-1]
    return pl.pallas_call(
        _w_in_layout_kernel,
        grid=(DEPTH, D_MODEL // rows),
        in_specs=[pl.BlockSpec((None, rows, in_cols), lambda l, i: (l, i, 0))],
        out_specs=pl.BlockSpec((None, rows, IN_COLS_PADDED), lambda l, i: (l, i, 0)),
        out_shape=jax.ShapeDtypeStruct((DEPTH, D_MODEL, IN_COLS_PADDED), BF16),
        compiler_params=_params(2),
        name="w_in_layout",
    )(w_in)


def _in_proj_kernel(latent, x_ref, mod_ref, perm_ref, gpre_ref, win_ref, gq_ref, wq_ref, gkv_ref, wkv_ref, *rest):
    if latent:
        cosq_ref, sinq_ref, ktab_ref = rest[:3]
        rest = rest[3:]
    q_ref, ckv_ref, kv_ref, kr_ref, lx_ref, gg_ref, gates_ref = rest
    sh1, sc1 = mod_ref[:, :, 0:D_MODEL], mod_ref[:, :, D_MODEL:2 * D_MODEL]
    hb3 = (_rms(x_ref[...], gpre_ref[...]) * (1.0 + sc1) + sh1).astype(BF16)
    hb = _flat(hb3)

    def proj(lhs, lo, hi):
        return _dot(lhs, win_ref[:, lo:hi])

    qn = _rms(proj(hb, C_Q, C_KV), gq_ref[...]).astype(BF16)
    qf = _dot(qn, wq_ref[...])
    if latent:
        cos, sin = _flat(cosq_ref[...]), _flat(sinq_ref[...])
        for h in range(N_HEADS):
            lanes = slice(h * HEAD_LANES, (h + 1) * HEAD_LANES)
            partner = slice(Q_COLS + h * HEAD_LANES, Q_COLS + (h + 1) * HEAD_LANES)
            q_ref[:, :, lanes] = _tiled((qf[:, lanes] * cos + qf[:, partner] * sin).astype(BF16))
    else:
        q_ref[...] = _tiled((qf * QK_DIM ** -0.5).astype(BF16))

    ckv = _rms(proj(hb, C_KV, C_LX), gkv_ref[...])
    ckv_ref[...] = _tiled(ckv)
    kv_ref[...] = _tiled(_dot(ckv.astype(BF16), wkv_ref[...]).astype(BF16))

    kr = proj(hb, C_KR, IN_COLS_PADDED)
    if latent:
        pr = kr * _flat(ktab_ref[...])
        kr = pr + pltpu.roll(pr, LANES - QK_ROPE, 1)
    kr_ref[...] = _tiled(kr[:, :QK_ROPE])

    gates_ref[...] = _tiled(_sigmoid(proj(hb, C_MG, C_KR)))

    hb_tm = _time_major_rows(perm_ref[...], hb3)
    lx_ref[...] = proj(hb_tm, C_LX, C_LG)
    gg_ref[...] = jax.nn.gelu(proj(hb_tm, C_LG, C_MG))


def _in_proj(grp, l, x, mod, perm, g_pre, w_in, g_q, w_q, g_kv, w_kv, rope_tabs):
    q_cols = w_q.shape[-1] if grp.latent else Q_COLS
    in_specs = [grp.seq_major(D_MODEL), grp.per_pseudo(N_MOD * D_MODEL, l), _resident(perm.shape),
                _resident((1, D_MODEL), l), _resident(w_in.shape[1:], l), _resident((1, Q_LORA), l),
                _resident((Q_LORA, q_cols), l), _resident((1, KV_LORA), l), _resident(w_kv.shape[1:], l)]
    args = [x, mod, perm, g_pre, w_in, g_q, w_q, g_kv, w_kv]
    if grp.latent:
        in_specs += [grp.by_position(HEAD_LANES), grp.by_position(HEAD_LANES), grp.by_position(LANES)]
        args += list(rope_tabs)
    return pl.pallas_call(
        functools.partial(_in_proj_kernel, grp.latent),
        grid=grp.grid,
        in_specs=in_specs,
        out_specs=[grp.seq_major(Q_COLS), grp.seq_major(KV_LORA), grp.seq_major(KV_COLS), grp.seq_major(QK_ROPE),
                   grp.time_major(LRU_WIDTH), grp.time_major(LRU_WIDTH), grp.seq_major(2 * D_MODEL)],
        out_shape=[grp.shape3(Q_COLS, BF16), grp.shape3(KV_LORA, F32), grp.shape3(KV_COLS, BF16),
                   grp.shape3(QK_ROPE, F32), grp.shape_tm(LRU_WIDTH, F32), grp.shape_tm(LRU_WIDTH, F32),
                   grp.shape3(2 * D_MODEL, F32)],
        compiler_params=_params(2),
        name="in_proj",
    )(*args)


def _kv_up_kernel(c_ref, w_ref, o_ref):
    o_ref[...] = _dot(c_ref[...].astype(BF16), w_ref[...]).astype(BF16)


def _kv_up(l, ckv, w_kv, rows_per_step):
    n = ckv.shape[0]
    return pl.pallas_call(
        _kv_up_kernel,
        grid=(n // rows_per_step,),
        in_specs=[pl.BlockSpec((rows_per_step, KV_LORA), lambda i: (i, 0)), _resident(w_kv.shape[1:], l)],
        out_specs=pl.BlockSpec((rows_per_step, KV_COLS), lambda i: (i, 0)),
        out_shape=jax.ShapeDtypeStruct((n, KV_COLS), BF16),
        compiler_params=_params(1),
        name="kv_up_ctx",
    )(ckv, w_kv)


def _attn_kernel(with_ctx, n_seq, q_ref, kv_ref, kr_ref, place_ref, *rest):
    if with_ctx:
        kvc_ref, krc_ref, o_ref = rest
    else:
        (o_ref,) = rest
    is_nope = lax.broadcasted_iota(jnp.int32, (1, HEAD_LANES), 1) < QK_NOPE

    def keys_of(kv, kr, rows):
        kr_placed = _dot(kr[rows, :].astype(BF16), place_ref[...]).astype(BF16)
        return lambda h: jnp.where(is_nope, kv[rows, h * HEAD_LANES:(h + 1) * HEAD_LANES], kr_placed)

    share = lambda ref, j: slice(j * (ref.shape[0] // n_seq), (j + 1) * (ref.shape[0] // n_seq))
    for j in range(n_seq):
        q_rows, k_rows = share(q_ref, j), share(kv_ref, j)
        keys = keys_of(kv_ref, kr_ref, k_rows)
        if with_ctx:
            c_rows = share(kvc_ref, j)
            keys_ctx = keys_of(kvc_ref, krc_ref, c_rows)
        for h in range(N_HEADS):
            lanes = slice(h * HEAD_LANES, (h + 1) * HEAD_LANES)
            q = q_ref[q_rows, lanes]
            s = _dot_t(q, keys(h))
            m = jnp.max(s, axis=-1, keepdims=True)
            if with_ctx:
                sc = _dot_t(q, keys_ctx(h))
                m = jnp.maximum(m, jnp.max(sc, axis=-1, keepdims=True))
            p = jnp.exp(s - m)
            den = jnp.sum(p, axis=-1, keepdims=True)
            o = _dot(p.astype(BF16), kv_ref[k_rows, lanes])
            if with_ctx:
                pc = jnp.exp(sc - m)
                den = den + jnp.sum(pc, axis=-1, keepdims=True)
                o = o + _dot(pc.astype(BF16), kvc_ref[c_rows, lanes])
            o_ref[q_rows, lanes] = (o / den).astype(BF16)


def _attention(grp, q, kv, kr, kv_ctx=None, kr_ctx=None):
    with_ctx = kv_ctx is not None
    q_block = min(Q_BLOCK, grp.seq_len)
    nq = grp.seq_len // q_block
    n_seq = ATTN_SEQS if nq == 1 and grp.batch % ATTN_SEQS == 0 else 1
    seq = lambda rows, width: pl.BlockSpec((n_seq * rows, width), lambda b, i: (b, 0))
    place = np.zeros((QK_ROPE, HEAD_LANES), np.float32)
    place[np.arange(QK_ROPE), QK_NOPE + np.arange(QK_ROPE)] = 1.0
    in_specs = [pl.BlockSpec((n_seq * q_block, Q_COLS), lambda b, i: (b * nq + i, 0)),
                seq(grp.seq_len, KV_COLS), seq(grp.seq_len, QK_ROPE), _resident(place.shape)]
    args = [q, kv, kr, jnp.asarray(place, BF16)]
    if with_ctx:
        past = kv_ctx.shape[0] // grp.batch
        in_specs += [seq(past, KV_COLS), seq(past, QK_ROPE)]
        args += [kv_ctx, kr_ctx]
    return pl.pallas_call(
        functools.partial(_attn_kernel, with_ctx, n_seq),
        grid=(grp.batch // n_seq, nq),
        in_specs=in_specs,
        out_specs=pl.BlockSpec((n_seq * q_block, ATTN_OUT), lambda b, i: (b * nq + i, 0)),
        out_shape=jax.ShapeDtypeStruct((grp.tokens, ATTN_OUT), BF16),
        compiler_params=_params(2),
        name="attention",
    )(*args)


def _softplus(z):
    return jnp.maximum(z, 0.0) + jnp.log1p(jnp.exp(-jnp.abs(z)))


def _lru_kernel(n_chunks, chunk_len, with_state, x_ref, gg_ref, cw_ref, cb_ref, wrg_ref, brg_ref,
                wig_ref, big_ref, lam_ref, hf0_ref, hb0_ref, *rest):
    if with_state:
        s_ref, hf_ref, hb_ref, xpad, a_f, u_f, a_b, u_b = rest
    else:
        s_ref, xpad, a_f, u_f, a_b, u_b = rest
    rows = chunk_len * SUBLANES
    halo = (CONV_W // 2) * SUBLANES
    sub = lax.broadcasted_iota(jnp.int32, (SUBLANES, LRU_CH), 0)
    first_chunk = (sub % n_chunks) == 0
    last_chunk = (sub % n_chunks) == n_chunks - 1

    def from_prev(v):
        return jnp.where(first_chunk, 0.0, pltpu.roll(v, 1, 0))

    def from_next(v):
        return jnp.where(last_chunk, 0.0, pltpu.roll(v, SUBLANES - 1, 0))

    xpad[halo:halo + rows, :] = x_ref[...]
    xpad[0:SUBLANES, :] = from_prev(x_ref[rows - 2 * SUBLANES:rows - SUBLANES, :])
    xpad[SUBLANES:halo, :] = from_prev(x_ref[rows - SUBLANES:rows, :])
    xpad[halo + rows:halo + rows + SUBLANES, :] = from_next(x_ref[0:SUBLANES, :])

    def gates(c, carry):
        r0 = pl.multiple_of(c * LRU_ROWS, LRU_ROWS)
        xc = cb_ref[...]
        for k in range(CONV_W):
            xc = xc + xpad[pl.ds(r0 + k * SUBLANES, LRU_ROWS), :] * cw_ref[k:k + 1, :]
        xcb = xc.astype(BF16)
        for d, (a_ref, u_ref) in enumerate(((a_f, u_f), (a_b, u_b))):
            for n in range(LRU_CH // LRU_BW):
                sl = slice(n * LRU_BW, (n + 1) * LRU_BW)
                t_r = jnp.tanh(_dot(xcb[:, sl], wrg_ref[d, n]) + brg_ref[d:d + 1, sl])
                t_i = jnp.tanh(_dot(xcb[:, sl], wig_ref[d, n]) + big_ref[d:d + 1, sl])
                half_c = (-0.5 * LRU_C) * _softplus(-lam_ref[d:d + 1, sl])
                log_a = half_c * t_r + half_c
                a = jnp.exp(log_a)
                a_ref[pl.ds(r0, LRU_ROWS), sl] = a
                one_minus_a2 = -jnp.tanh(log_a) * (a * a + 1.0)
                root = jnp.where(one_minus_a2 > 0.0, one_minus_a2 * lax.rsqrt(one_minus_a2), 0.0)
                u_ref[pl.ds(r0, LRU_ROWS), sl] = root * (0.5 * t_i + 0.5) * xc[:, sl]
        return carry

    lax.fori_loop(0, rows // LRU_ROWS, gates, 0)

    block_rows = SCAN_BLOCK * SUBLANES

    def advance(a_ref, u_ref, first_step, order, h, store):
        slab = pl.ds(pl.multiple_of(first_step * SUBLANES, block_rows), block_rows)
        a_all, u_all = a_ref[slab, :], u_ref[slab, :]
        a = [a_all[j * SUBLANES:(j + 1) * SUBLANES] for j in order]
        u = [u_all[j * SUBLANES:(j + 1) * SUBLANES] for j in order]
        a01, u01 = a[1] * a[0], a[1] * u[0] + u[1]
        a23, u23 = a[3] * a[2], a[3] * u[2] + u[3]
        a03, u03 = a23 * a01, a23 * u01 + u23
        h4 = a03 * h + u03
        if store:
            h2 = a01 * h + u01
            states = dict(zip(order, (a[0] * h + u[0], h2, a[2] * h2 + u[2], h4)))
            u_ref[slab, :] = jnp.concatenate([states[j] for j in range(SCAN_BLOCK)], axis=0)
        return h4

    def scan(hf, hb, store):
        ascending, descending = tuple(range(SCAN_BLOCK)), tuple(reversed(range(SCAN_BLOCK)))

        def block(k, carry):
            hf, hb = carry
            hf = advance(a_f, u_f, k * SCAN_BLOCK, ascending, hf, store)
            hb = advance(a_b, u_b, chunk_len - (k + 1) * SCAN_BLOCK, descending, hb, store)
            return hf, hb
        return lax.fori_loop(0, chunk_len // SCAN_BLOCK, block, (hf, hb), unroll=SCAN_UNROLL)

    hf, hb = hf0_ref[...], hb0_ref[...]
    if n_chunks > 1:
        ef, eb = scan(hf, hb, False)
        hf = jnp.where(first_chunk, hf, pltpu.roll(ef, 1, 0))
        hb = jnp.where(last_chunk, hb, pltpu.roll(eb, SUBLANES - 1, 0))
    hf, hb = scan(hf, hb, True)
    if with_state:
        hf_ref[...] = hf
        hb_ref[...] = hb

    def combine(c, carry):
        r = pl.ds(pl.multiple_of(c * LRU_ROWS, LRU_ROWS), LRU_ROWS)
        s_ref[r, :] = ((u_f[r, :] + u_b[r, :]) * gg_ref[r, :]).astype(BF16)
        return carry

    lax.fori_loop(0, rows // LRU_ROWS, combine, 0)


def _rg_lru(grp, l, lx, gg, conv_w, conv_b, w_rg, b_rg, w_ig, b_ig, lam, hf0, hb0, with_state):
    assert grp.n_chunks == 1 or not with_state
    rows = grp.chunk_len * SUBLANES
    nb = LRU_CH // LRU_BW
    tile = pl.BlockSpec((rows, LRU_CH), lambda g, c: (g, c))
    chan = lambda r: pl.BlockSpec((None, r, LRU_CH), lambda g, c: (l, 0, c))
    wblk = pl.BlockSpec((None, 2, nb, LRU_BW, LRU_BW), lambda g, c: (l, 0, c, 0, 0))
    state = pl.BlockSpec((SUBLANES, LRU_CH), lambda g, c: (g, c))
    out_specs = [tile]
    out_shape = [grp.shape_tm(LRU_WIDTH, BF16)]
    if with_state:
        out_specs += [state, state]
        out_shape += [jax.ShapeDtypeStruct((grp.n_pseudo, LRU_WIDTH), F32)] * 2
    scratch = [pltpu.VMEM((rows + (CONV_W - 1) * SUBLANES, LRU_CH), F32)] + [pltpu.VMEM((rows, LRU_CH), F32)] * 4
    return pl.pallas_call(
        functools.partial(_lru_kernel, grp.n_chunks, grp.chunk_len, with_state),
        grid=(grp.lru_groups, LRU_WIDTH // LRU_CH),
        in_specs=[tile, tile, chan(CONV_W), chan(1), wblk, chan(2), wblk, chan(2), chan(2), state, state],
        out_specs=out_specs,
        out_shape=out_shape,
        scratch_shapes=scratch,
        compiler_params=_params(2),
        name="rg_lru",
    )(lx, gg, conv_w, conv_b, w_rg, b_rg, w_ig, b_ig, lam, hf0, hb0)


def _merge_kernel(o_ref, s_ref, gates_ref, x_ref, mod_ref, perm_ref, gpost_ref, wom_ref, wol_ref, wout_ref, y_ref):
    y_mla = _dot(_flat(o_ref[...]), wom_ref[...])
    s = _flat(_seq_major_tile(perm_ref[...], s_ref[...]))
    y_lru = _dot(s, wol_ref[...])
    gates = _flat(gates_ref[...])
    z = gates[:, :D_MODEL] * y_mla + gates[:, D_MODEL:] * y_lru
    y = _dot(z.astype(BF16), wout_ref[...])
    g1 = mod_ref[:, :, 2 * D_MODEL:3 * D_MODEL]
    y_ref[...] = x_ref[...] + g1 * _tiled(_rms(y, gpost_ref[...]))


def _merge(grp, l, o, s, gates, x, mod, perm_t, g_post, w_o_mla, w_o_lru, w_out):
    return pl.pallas_call(
        _merge_kernel,
        grid=grp.grid,
        in_specs=[grp.seq_major(ATTN_OUT), grp.time_major(LRU_WIDTH), grp.seq_major(2 * D_MODEL),
                  grp.seq_major(D_MODEL), grp.per_pseudo(N_MOD * D_MODEL, l), _resident(perm_t.shape),
                  _resident((1, D_MODEL), l), _resident(w_o_mla.shape[1:], l), _resident(w_o_lru.shape[1:], l),
                  _resident(w_out.shape[1:], l)],
        out_specs=grp.seq_major(D_MODEL),
        out_shape=grp.shape3(D_MODEL, F32),
        compiler_params=_params(2),
        name="merge",
    )(o, s, gates, x, mod, perm_t, g_post, w_o_mla, w_o_lru, w_out)


def _ffn_kernel(x_ref, mod_ref, gpre_ref, gpost_ref, w1_ref, w2_ref, y_ref, act_ref):
    x = x_ref[...]
    sh2, sc2, g2 = (mod_ref[:, :, k * D_MODEL:(k + 1) * D_MODEL] for k in (3, 4, 5))
    hb = _flat((_rms(x, gpre_ref[...]) * (1.0 + sc2) + sh2).astype(BF16))
    for c in range(FF_HIDDEN // FFN_CHUNK):
        lo = c * FFN_CHUNK
        g = _dot(hb, w1_ref[:, lo:lo + FFN_CHUNK])
        u = _dot(hb, w1_ref[:, FF_HIDDEN + lo:FF_HIDDEN + lo + FFN_CHUNK])
        act_ref[:, lo:lo + FFN_CHUNK] = (g * _sigmoid(g) * u).astype(BF16)
    y = _dot(act_ref[...], w2_ref[...])
    y_ref[...] = x + g2 * _tiled(_rms(y, gpost_ref[...]))


def _ffn(grp, l, x, mod, g_pre, g_post, w1, w2):
    assert grp.chunk_len % FFN_STEPS == 0
    return pl.pallas_call(
        _ffn_kernel,
        grid=(grp.lru_groups, grp.chunk_len // FFN_STEPS),
        in_specs=[grp.seq_major(D_MODEL, FFN_STEPS), grp.per_pseudo(N_MOD * D_MODEL, l),
                  _resident((1, D_MODEL), l), _resident((1, D_MODEL), l), _resident(w1.shape[1:], l),
                  _resident(w2.shape[1:], l)],
        out_specs=grp.seq_major(D_MODEL, FFN_STEPS),
        out_shape=grp.shape3(D_MODEL, F32),
        scratch_shapes=[pltpu.VMEM((SUBLANES * FFN_STEPS, FF_HIDDEN), BF16)],
        compiler_params=_params(2),
        name="ffn",
    )(x, mod, g_pre, g_post, w1, w2)


def _merge_ffn_kernel(o_ref, s_ref, gates_ref, x_ref, mod_ref, perm_ref, gpost_mix_ref, wom_ref, wol_ref, wout_ref,
                      gpre_ref, gpost_ref, w1_ref, w2_ref, y_ref, act_ref):
    y_mla = _dot(_flat(o_ref[...]), wom_ref[...])
    s = _flat(_seq_major_tile(perm_ref[...], s_ref[...]))
    y_lru = _dot(s, wol_ref[...])
    z = gates_ref[:, :, :D_MODEL] * _tiled(y_mla) + gates_ref[:, :, D_MODEL:] * _tiled(y_lru)
    y = _dot(_flat(z.astype(BF16)), wout_ref[...])
    g1 = mod_ref[:, :, 2 * D_MODEL:3 * D_MODEL]
    x = x_ref[...] + g1 * _tiled(_rms(y, gpost_mix_ref[...]))
    sh2, sc2, g2 = (mod_ref[:, :, k * D_MODEL:(k + 1) * D_MODEL] for k in (3, 4, 5))
    hb = _flat((_rms(x, gpre_ref[...]) * (1.0 + sc2) + sh2).astype(BF16))
    for c in range(FF_HIDDEN // FFN_CHUNK):
        lo = c * FFN_CHUNK
        g = _dot(hb, w1_ref[:, lo:lo + FFN_CHUNK])
        u = _dot(hb, w1_ref[:, FF_HIDDEN + lo:FF_HIDDEN + lo + FFN_CHUNK])
        act_ref[:, lo:lo + FFN_CHUNK] = (g * _sigmoid(g) * u).astype(BF16)
    y2 = _dot(act_ref[...], w2_ref[...])
    y_ref[...] = x + g2 * _tiled(_rms(y2, gpost_ref[...]))


def _merge_ffn(grp, l, o, s, gates, x, mod, perm_t, g_post_mix, w_o_mla, w_o_lru, w_out, g_pre, g_post, w1, w2):
    return pl.pallas_call(
        _merge_ffn_kernel,
        grid=grp.grid,
        in_specs=[grp.seq_major(ATTN_OUT), grp.time_major(LRU_WIDTH), grp.seq_major(2 * D_MODEL),
                  grp.seq_major(D_MODEL), grp.per_pseudo(N_MOD * D_MODEL, l), _resident(perm_t.shape),
                  _resident((1, D_MODEL), l), _resident(w_o_mla.shape[1:], l), _resident(w_o_lru.shape[1:], l),
                  _resident(w_out.shape[1:], l), _resident((1, D_MODEL), l), _resident((1, D_MODEL), l),
                  _resident(w1.shape[1:], l), _resident(w2.shape[1:], l)],
        out_specs=grp.seq_major(D_MODEL),
        out_shape=grp.shape3(D_MODEL, F32),
        scratch_shapes=[pltpu.VMEM((TILE_ROWS, FF_HIDDEN), BF16)],
        compiler_params=pltpu.CompilerParams(dimension_semantics=("arbitrary", "arbitrary"),
                                             vmem_limit_bytes=MERGE_FFN_VMEM_BYTES),
        name="merge_ffn",
    )(o, s, gates, x, mod, perm_t, g_post_mix, w_o_mla, w_o_lru, w_out, g_pre, g_post, w1, w2)


def _rot_partner(w):
    half = QK_ROPE // 2
    return jnp.concatenate([-w[..., half:], w[..., :half]], axis=-1)


def _rope_tables(grp):
    n_tokens = grp.seq_len
    rows = n_tokens // GRID_W
    row = jnp.repeat(jnp.arange(rows, dtype=F32), GRID_W)
    col = jnp.tile(jnp.arange(GRID_W, dtype=F32), rows)
    n_freq = QK_ROPE // 4
    inv = ROPE_THETA ** (-jnp.arange(n_freq, dtype=F32) / n_freq)
    ang = jnp.concatenate([row[:, None] * inv, col[:, None] * inv], axis=-1)
    cos, sin = jnp.cos(ang), jnp.sin(ang)
    cos2, sin2 = jnp.concatenate([cos, cos], -1), jnp.concatenate([sin, sin], -1)
    ktab = jnp.concatenate([cos2, sin2, jnp.zeros((n_tokens, LANES - 2 * QK_ROPE), F32)], -1)
    scale = QK_DIM ** -0.5
    tail = jnp.zeros((n_tokens, HEAD_LANES - QK_DIM), F32)
    qcos = jnp.concatenate([jnp.ones((n_tokens, QK_NOPE), F32), cos2, tail], -1) * scale
    qsin = jnp.concatenate([jnp.zeros((n_tokens, QK_NOPE), F32), sin2, tail], -1) * scale

    def by_pseudo(tab):
        tab = tab.reshape(grp.n_chunks, grp.chunk_len, tab.shape[-1])
        return jnp.tile(tab, (SUBLANES // grp.n_chunks, 1, 1))

    return by_pseudo(qcos), by_pseudo(qsin), by_pseudo(ktab)


def kernel(x_prompt, x_sample, cache_ckv, cache_krope, state_lru_fwd, state_lru_bwd, c, c_ctx, w_mod, b_mod, g_pre_mix, g_post_mix, g_pre_ffn, g_post_ffn, w_in, g_q, w_q_up, g_kv, w_kv_up, w_o_mla, conv_w, conv_b, w_rg, b_rg, w_ig, b_ig, lru_lambda, w_o_lru, w_out, w_ffn_in, w_ffn_out):
    batch, seq = x_prompt.shape[:2]
    dec_batch, dec_seq = x_sample.shape[:2]
    past = cache_ckv.shape[2]

    w_in_r = _w_in_layout(w_in)
    wq = w_q_up.reshape(DEPTH, Q_LORA, N_HEADS, QK_DIM)
    q_tail = jnp.zeros((DEPTH, Q_LORA, N_HEADS, HEAD_LANES - QK_DIM), F32)
    q_main = jnp.concatenate([wq, q_tail], axis=-1)
    q_partner = jnp.concatenate([jnp.zeros_like(wq[..., :QK_NOPE]), _rot_partner(wq[..., QK_NOPE:]), q_tail], axis=-1)
    w_q_r = jnp.concatenate([q_main.reshape(DEPTH, Q_LORA, Q_COLS),
                             q_partner.reshape(DEPTH, Q_LORA, Q_COLS)], axis=-1).astype(BF16)
    w_kv_r = w_kv_up.astype(BF16)
    wo = w_o_mla.reshape(DEPTH, N_HEADS, V_DIM, D_MODEL)
    w_o_mla_b = jnp.concatenate([jnp.zeros((DEPTH, N_HEADS, HEAD_LANES - V_DIM, D_MODEL), F32), wo],
                                axis=2).reshape(DEPTH, ATTN_OUT, D_MODEL).astype(BF16)
    w_o_lru_b, w_out_b = w_o_lru.astype(BF16), w_out.astype(BF16)
    w_rg_b, w_ig_b = (0.5 * w_rg).astype(BF16), (0.5 * w_ig).astype(BF16)
    b_rg, b_ig = 0.5 * b_rg, 0.5 * b_ig
    w1_b, w2_b = w_ffn_in.astype(BF16), w_ffn_out.astype(BF16)
    row = lambda v: v.reshape(DEPTH, 1, -1)
    g_pre_mix, g_post_mix, g_pre_ffn, g_post_ffn = map(row, (g_pre_mix, g_post_mix, g_pre_ffn, g_post_ffn))
    g_q, g_kv, conv_b = row(g_q), row(g_kv), row(conv_b)
    perm = _to_time_major()
    perm_b, perm_t_b = jnp.asarray(perm, BF16), jnp.asarray(perm.T, BF16)

    mod_all = _modulation(jnp.concatenate([c_ctx[None, :], c], axis=0), w_mod, b_mod)

    def run_group(grp, x, mod_rows, layer_inputs):
        x = x.reshape(grp.n_pseudo, grp.chunk_len, D_MODEL)
        rope_tabs = _rope_tables(grp) if grp.latent else None
        flat = lambda a: a.reshape(grp.tokens, a.shape[-1])
        mod = mod_all[:, mod_rows][:, :, None, :]
        per_layer = []
        for l in range(DEPTH):
            q, ckv, kv, kr, lx, gg, gates = _in_proj(grp, l, x, mod, perm_b, g_pre_mix, w_in_r, g_q, w_q_r,
                                                     g_kv, w_kv_r, rope_tabs)
            hf0, hb0, kv_ctx, kr_ctx = layer_inputs(l)
            o = _attention(grp, flat(q), flat(kv), flat(kr), kv_ctx, kr_ctx)
            lru = _rg_lru(grp, l, lx, gg, conv_w, conv_b, w_rg_b, b_rg, w_ig_b, b_ig, lru_lambda,
                          hf0, hb0, not grp.latent)
            o = o.reshape(grp.n_pseudo, grp.chunk_len, ATTN_OUT)
            x = _merge_ffn(grp, l, o, lru[0], gates, x, mod, perm_t_b, g_post_mix, w_o_mla_b, w_o_lru_b, w_out_b,
                           g_pre_ffn, g_post_ffn, w1_b, w2_b)
            per_layer.append((ckv, kr) + tuple(lru[1:]))
        return x, per_layer

    ctx = _Group(batch, seq, latent=False)
    zeros_state = jnp.zeros((batch, LRU_WIDTH), F32)
    y_prompt, ctx_layers = run_group(ctx, x_prompt, np.zeros(ctx.n_pseudo, np.int32),
                                     lambda l: (zeros_state, zeros_state, None, None))
    stack = lambda k, shape: jnp.stack([lay[k].reshape(shape) for lay in ctx_layers], axis=1)
    new_ckv = stack(0, (batch, seq, KV_LORA))
    new_krope = stack(1, (batch, seq, QK_ROPE))
    new_lru_fwd = stack(2, (batch, LRU_WIDTH))
    new_lru_bwd = stack(3, (batch, LRU_WIDTH))

    lat = _Group(dec_batch, dec_seq, latent=True)

    def latent_inputs(l):
        per_chunk = lambda s: jnp.repeat(s[:, l], lat.n_chunks, axis=0)
        kv_ctx = _kv_up(l, cache_ckv[:, l].reshape(dec_batch * past, KV_LORA), w_kv_r, past)
        return (per_chunk(state_lru_fwd), per_chunk(state_lru_bwd), kv_ctx,
                cache_krope[:, l].reshape(dec_batch * past, QK_ROPE))

    y_sample, _ = run_group(lat, x_sample, 1 + np.arange(lat.n_pseudo) // lat.n_chunks, latent_inputs)
    return (y_prompt.reshape(batch, seq, D_MODEL), y_sample.reshape(dec_batch, dec_seq, D_MODEL),
            new_ckv, new_krope, new_lru_fwd, new_lru_bwd)
```

```python
import functools

import numpy as np

import jax
import jax.numpy as jnp
from jax import lax
from jax.experimental import pallas as pl
from jax.experimental.pallas import tpu as pltpu

D_MODEL = 1024
DEPTH = 2
GRID_W = 64
N_HEADS = 8
QK_NOPE = 64
QK_ROPE = 32
QK_DIM = QK_NOPE + QK_ROPE
V_DIM = 64
Q_LORA = 384
KV_LORA = 256
ROPE_THETA = 10000.0
LRU_WIDTH = 1024
LRU_BLOCKS = 8
LRU_BW = LRU_WIDTH // LRU_BLOCKS
CONV_W = 4
LRU_C = 8.0
FF_HIDDEN = 2816
N_MOD = 6
EPS = 1e-6

SUBLANES = 8
LANES = 128
VMEM_LIMIT_BYTES = 56 * 1024 * 1024
MERGE_FFN_VMEM_BYTES = 60 * 1024 * 1024

C_Q = 0
C_KV = C_Q + Q_LORA
C_LX = C_KV + KV_LORA
C_LG = C_LX + LRU_WIDTH
C_MG = C_LG + LRU_WIDTH
C_KR = C_MG + 2 * D_MODEL
IN_COLS_PADDED = C_KR + LANES
HEAD_LANES = LANES
assert QK_DIM <= HEAD_LANES and QK_NOPE + V_DIM == HEAD_LANES
Q_COLS = N_HEADS * HEAD_LANES
KV_COLS = N_HEADS * HEAD_LANES
ATTN_OUT = N_HEADS * HEAD_LANES

STEPS = 64
TILE_ROWS = SUBLANES * STEPS
PERM_STEPS = 32
PERM_ROWS = SUBLANES * PERM_STEPS
assert STEPS % PERM_STEPS == 0
Q_BLOCK = 1024
ATTN_SEQS = 8
FFN_STEPS = 128
FFN_CHUNK = 256
LRU_CH = 256
LRU_ROWS = 512
SCAN_BLOCK = 4
SCAN_UNROLL = 2
MOD_TILE = 1536

F32 = jnp.float32
BF16 = jnp.bfloat16


def _params(n_grid_dims):
    return pltpu.CompilerParams(dimension_semantics=("arbitrary",) * n_grid_dims,
                                vmem_limit_bytes=VMEM_LIMIT_BYTES)


def _resident(shape, layer=None):
    zeros = (0,) * len(shape)
    if layer is None:
        return pl.BlockSpec(shape, lambda *_: zeros, pipeline_mode=pl.Buffered(1))
    return pl.BlockSpec((None,) + tuple(shape), lambda *_: (layer,) + zeros, pipeline_mode=pl.Buffered(1))


def _rms(x, g):
    return x * lax.rsqrt(jnp.mean(x * x, axis=-1, keepdims=True) + EPS) * g


def _dot(a, b):
    return jnp.dot(a, b, preferred_element_type=F32)


def _dot_t(a, b):
    return lax.dot_general(a, b, (((1,), (1,)), ((), ())), preferred_element_type=F32)


def _flat(x3):
    return x3.reshape(x3.shape[0] * x3.shape[1], x3.shape[-1])


def _tiled(x2):
    return x2.reshape(SUBLANES, x2.shape[0] // SUBLANES, x2.shape[-1])


def _sigmoid(x):
    return 0.5 * jnp.tanh(0.5 * x) + 0.5


def _to_time_major():
    r = np.arange(PERM_ROWS)
    perm = np.zeros((PERM_ROWS, PERM_ROWS), np.float32)
    perm[r, (r % SUBLANES) * PERM_STEPS + r // SUBLANES] = 1.0
    return perm


def _time_major_rows(perm, x3):
    parts = [_dot(perm, _flat(x3[:, t:t + PERM_STEPS, :])).astype(BF16) for t in range(0, x3.shape[1], PERM_STEPS)]
    return jnp.concatenate(parts, axis=0)


def _seq_major_tile(perm_t, x2):
    parts = [_tiled(_dot(perm_t, x2[r:r + PERM_ROWS, :]).astype(BF16)) for r in range(0, x2.shape[0], PERM_ROWS)]
    return jnp.concatenate(parts, axis=1)


class _Group:
    def __init__(self, batch, seq_len, latent):
        self.batch, self.seq_len, self.latent = batch, seq_len, latent
        self.tokens = batch * seq_len
        self.n_chunks = max(1, SUBLANES // batch)
        self.n_pseudo = batch * self.n_chunks
        assert self.n_chunks in (1, 2) and self.n_pseudo % SUBLANES == 0
        self.chunk_len = seq_len // self.n_chunks
        assert self.chunk_len % STEPS == 0 and seq_len % min(Q_BLOCK, seq_len) == 0
        self.lru_groups = self.n_pseudo // SUBLANES
        self.tiles_per_chunk = self.chunk_len // STEPS
        self.grid = (self.lru_groups, self.tiles_per_chunk)

    def seq_major(self, width, steps=STEPS):
        return pl.BlockSpec((SUBLANES, steps, width), lambda g, i: (g, i, 0))

    def time_major(self, width):
        return pl.BlockSpec((TILE_ROWS, width), lambda g, i: (g * self.tiles_per_chunk + i, 0))

    def per_pseudo(self, width, layer):
        return pl.BlockSpec((None, SUBLANES, 1, width), lambda g, i: (layer, g, 0, 0))

    def by_position(self, width):
        return pl.BlockSpec((SUBLANES, STEPS, width), lambda g, i: (0, i, 0))

    def shape3(self, width, dtype):
        return jax.ShapeDtypeStruct((self.n_pseudo, self.chunk_len, width), dtype)

    def shape_tm(self, width, dtype):
        return jax.ShapeDtypeStruct((self.lru_groups * self.chunk_len * SUBLANES, width), dtype)


def _mod_kernel(c_ref, w_ref, b_ref, o_ref):
    c = c_ref[...]
    a = (c * _sigmoid(c)).astype(BF16)
    o_ref[...] = _dot(a, w_ref[...].astype(BF16)) + b_ref[...]


def _modulation(cond, w_mod, b_mod):
    n = N_MOD * D_MODEL
    rows = cond.shape[0]
    return pl.pallas_call(
        _mod_kernel,
        grid=(DEPTH, n // MOD_TILE),
        in_specs=[pl.BlockSpec((rows, D_MODEL), lambda l, j: (0, 0)),
                  pl.BlockSpec((None, D_MODEL, MOD_TILE), lambda l, j: (l, 0, j)),
                  pl.BlockSpec((None, 1, MOD_TILE), lambda l, j: (l, 0, j))],
        out_specs=pl.BlockSpec((None, rows, MOD_TILE), lambda l, j: (l, 0, j)),
        out_shape=jax.ShapeDtypeStruct((DEPTH, rows, n), F32),
        compiler_params=_params(2),
        name="modulation",
    )(cond, w_mod, b_mod.reshape(DEPTH, 1, n))


def _w_in_layout_kernel(w_ref, o_ref):
    off_kr = Q_LORA + KV_LORA
    lane = lax.broadcasted_iota(jnp.int32, (1, LANES), 1)
    half = QK_ROPE // 2
    o_ref[:, C_Q:C_LX] = w_ref[:, :off_kr].astype(BF16)
    o_ref[:, C_LX:C_KR] = w_ref[:, off_kr + QK_ROPE:].astype(BF16)
    tile = w_ref[:, off_kr:off_kr + LANES]
    partner = jnp.where(lane < half, -pltpu.roll(tile, LANES - half, 1), pltpu.roll(tile, half, 1))
    seg = jnp.where(lane < QK_ROPE, tile, jnp.where(lane < 2 * QK_ROPE, pltpu.roll(partner, QK_ROPE, 1), 0.0))
    o_ref[:, C_KR:IN_COLS_PADDED] = seg.astype(BF16)


def _w_in_layout(w_in):
    rows = 256
    in_cols = w_in.shape[-1]
    return pl.pallas_call(
        _w_in_layout_kernel,
        grid=(DEPTH, D_MODEL // rows),
        in_specs=[pl.BlockSpec((None, rows, in_cols), lambda l, i: (l, i, 0))],
        out_specs=pl.BlockSpec((None, rows, IN_COLS_PADDED), lambda l, i: (l, i, 0)),
        out_shape=jax.ShapeDtypeStruct((DEPTH, D_MODEL, IN_COLS_PADDED), BF16),
        compiler_params=_params(2),
        name="w_in_layout",
    )(w_in)


def _in_proj_kernel(latent, x_ref, mod_ref, perm_ref, gpre_ref, win_ref, gq_ref, wq_ref, gkv_ref, wkv_ref, *rest):
    if latent:
        cosq_ref, sinq_ref, ktab_ref = rest[:3]
        rest = rest[3:]
    q_ref, ckv_ref, kv_ref, kr_ref, lx_ref, gg_ref, gates_ref = rest
    sh1, sc1 = mod_ref[:, :, 0:D_MODEL], mod_ref[:, :, D_MODEL:2 * D_MODEL]
    hb3 = (_rms(x_ref[...], gpre_ref[...]) * (1.0 + sc1) + sh1).astype(BF16)
    hb = _flat(hb3)

    def proj(lhs, lo, hi):
        return _dot(lhs, win_ref[:, lo:hi])

    lat = proj(hb, C_Q, C_LX)
    qn = _rms(lat[:, :C_KV], gq_ref[...]).astype(BF16)
    qf = _dot(qn, wq_ref[...])
    if latent:
        cos, sin = _flat(cosq_ref[...]), _flat(sinq_ref[...])
        for h in range(N_HEADS):
            lanes = slice(h * HEAD_LANES, (h + 1) * HEAD_LANES)
            partner = slice(Q_COLS + h * HEAD_LANES, Q_COLS + (h + 1) * HEAD_LANES)
            q_ref[:, :, lanes] = _tiled((qf[:, lanes] * cos + qf[:, partner] * sin).astype(BF16))
    else:
        q_ref[...] = _tiled((qf * QK_DIM ** -0.5).astype(BF16))

    ckv = _rms(lat[:, C_KV:], gkv_ref[...])
    ckv_ref[...] = _tiled(ckv)
    kv_ref[...] = _tiled(_dot(ckv.astype(BF16), wkv_ref[...]).astype(BF16))

    kr = proj(hb, C_KR, IN_COLS_PADDED)
    if latent:
        pr = kr * _flat(ktab_ref[...])
        kr = pr + pltpu.roll(pr, LANES - QK_ROPE, 1)
    kr_ref[...] = _tiled(kr[:, :QK_ROPE])

    gates_ref[...] = _tiled(_sigmoid(proj(hb, C_MG, C_KR)))

    hb_tm = _time_major_rows(perm_ref[...], hb3)
    lx_ref[...] = proj(hb_tm, C_LX, C_LG)
    gg_ref[...] = jax.nn.gelu(proj(hb_tm, C_LG, C_MG))


def _in_proj(grp, l, x, mod, perm, g_pre, w_in, g_q, w_q, g_kv, w_kv, rope_tabs):
    q_cols = w_q.shape[-1] if grp.latent else Q_COLS
    in_specs = [grp.seq_major(D_MODEL), grp.per_pseudo(N_MOD * D_MODEL, l), _resident(perm.shape),
                _resident((1, D_MODEL), l), _resident(w_in.shape[1:], l), _resident((1, Q_LORA), l),
                _resident((Q_LORA, q_cols), l), _resident((1, KV_LORA), l), _resident(w_kv.shape[1:], l)]
    args = [x, mod, perm, g_pre, w_in, g_q, w_q, g_kv, w_kv]
    if grp.latent:
        in_specs += [grp.by_position(HEAD_LANES), grp.by_position(HEAD_LANES), grp.by_position(LANES)]
        args += list(rope_tabs)
    return pl.pallas_call(
        functools.partial(_in_proj_kernel, grp.latent),
        grid=grp.grid,
        in_specs=in_specs,
        out_specs=[grp.seq_major(Q_COLS), grp.seq_major(KV_LORA), grp.seq_major(KV_COLS), grp.seq_major(QK_ROPE),
                   grp.time_major(LRU_WIDTH), grp.time_major(LRU_WIDTH), grp.seq_major(2 * D_MODEL)],
        out_shape=[grp.shape3(Q_COLS, BF16), grp.shape3(KV_LORA, F32), grp.shape3(KV_COLS, BF16),
                   grp.shape3(QK_ROPE, F32), grp.shape_tm(LRU_WIDTH, F32), grp.shape_tm(LRU_WIDTH, F32),
                   grp.shape3(2 * D_MODEL, F32)],
        compiler_params=_params(2),
        name="in_proj",
    )(*args)


def _kv_up_kernel(c_ref, w_ref, o_ref):
    o_ref[...] = _dot(c_ref[...].astype(BF16), w_ref[...]).astype(BF16)


def _kv_up(l, ckv, w_kv, rows_per_step):
    n = ckv.shape[0]
    return pl.pallas_call(
        _kv_up_kernel,
        grid=(n // rows_per_step,),
        in_specs=[pl.BlockSpec((rows_per_step, KV_LORA), lambda i: (i, 0)), _resident(w_kv.shape[1:], l)],
        out_specs=pl.BlockSpec((rows_per_step, KV_COLS), lambda i: (i, 0)),
        out_shape=jax.ShapeDtypeStruct((n, KV_COLS), BF16),
        compiler_params=_params(1),
        name="kv_up_ctx",
    )(ckv, w_kv)


def _attn_kernel(with_ctx, n_seq, q_ref, kv_ref, kr_ref, place_ref, *rest):
    if with_ctx:
        kvc_ref, krc_ref, o_ref = rest
    else:
        (o_ref,) = rest
    is_nope = lax.broadcasted_iota(jnp.int32, (1, HEAD_LANES), 1) < QK_NOPE

    def keys_of(kv, kr, rows):
        kr_placed = _dot(kr[rows, :].astype(BF16), place_ref[...]).astype(BF16)
        return lambda h: jnp.where(is_nope, kv[rows, h * HEAD_LANES:(h + 1) * HEAD_LANES], kr_placed)

    share = lambda ref, j: slice(j * (ref.shape[0] // n_seq), (j + 1) * (ref.shape[0] // n_seq))
    for j in range(n_seq):
        q_rows, k_rows = share(q_ref, j), share(kv_ref, j)
        keys = keys_of(kv_ref, kr_ref, k_rows)
        if with_ctx:
            c_rows = share(kvc_ref, j)
            keys_ctx = keys_of(kvc_ref, krc_ref, c_rows)
        for h in range(N_HEADS):
            lanes = slice(h * HEAD_LANES, (h + 1) * HEAD_LANES)
            q = q_ref[q_rows, lanes]
            s = _dot_t(q, keys(h))
            m = jnp.max(s, axis=-1, keepdims=True)
            if with_ctx:
                sc = _dot_t(q, keys_ctx(h))
                m = jnp.maximum(m, jnp.max(sc, axis=-1, keepdims=True))
            p = jnp.exp(s - m)
            den = jnp.sum(p, axis=-1, keepdims=True)
            o = _dot(p.astype(BF16), kv_ref[k_rows, lanes])
            if with_ctx:
                pc = jnp.exp(sc - m)
                den = den + jnp.sum(pc, axis=-1, keepdims=True)
                o = o + _dot(pc.astype(BF16), kvc_ref[c_rows, lanes])
            o_ref[q_rows, lanes] = (o / den).astype(BF16)


def _attention(grp, q, kv, kr, kv_ctx=None, kr_ctx=None):
    with_ctx = kv_ctx is not None
    q_block = min(Q_BLOCK, grp.seq_len)
    nq = grp.seq_len // q_block
    n_seq = ATTN_SEQS if nq == 1 and grp.batch % ATTN_SEQS == 0 else 1
    seq = lambda rows, width: pl.BlockSpec((n_seq * rows, width), lambda b, i: (b, 0))
    place = np.zeros((QK_ROPE, HEAD_LANES), np.float32)
    place[np.arange(QK_ROPE), QK_NOPE + np.arange(QK_ROPE)] = 1.0
    in_specs = [pl.BlockSpec((n_seq * q_block, Q_COLS), lambda b, i: (b * nq + i, 0)),
                seq(grp.seq_len, KV_COLS), seq(grp.seq_len, QK_ROPE), _resident(place.shape)]
    args = [q, kv, kr, jnp.asarray(place, BF16)]
    if with_ctx:
        past = kv_ctx.shape[0] // grp.batch
        in_specs += [seq(past, KV_COLS), seq(past, QK_ROPE)]
        args += [kv_ctx, kr_ctx]
    return pl.pallas_call(
        functools.partial(_attn_kernel, with_ctx, n_seq),
        grid=(grp.batch // n_seq, nq),
        in_specs=in_specs,
        out_specs=pl.BlockSpec((n_seq * q_block, ATTN_OUT), lambda b, i: (b * nq + i, 0)),
        out_shape=jax.ShapeDtypeStruct((grp.tokens, ATTN_OUT), BF16),
        compiler_params=_params(2),
        name="attention",
    )(*args)


def _softplus(z):
    return jnp.maximum(z, 0.0) + jnp.log1p(jnp.exp(-jnp.abs(z)))


def _lru_kernel(n_chunks, chunk_len, with_state, x_ref, gg_ref, cw_ref, cb_ref, wrg_ref, brg_ref,
                wig_ref, big_ref, lam_ref, hf0_ref, hb0_ref, *rest):
    if with_state:
        s_ref, hf_ref, hb_ref, xpad, a_f, u_f, a_b, u_b = rest
    else:
        s_ref, xpad, a_f, u_f, a_b, u_b = rest
    rows = chunk_len * SUBLANES
    halo = (CONV_W // 2) * SUBLANES
    sub = lax.broadcasted_iota(jnp.int32, (SUBLANES, LRU_CH), 0)
    first_chunk = (sub % n_chunks) == 0
    last_chunk = (sub % n_chunks) == n_chunks - 1

    def from_prev(v):
        return jnp.where(first_chunk, 0.0, pltpu.roll(v, 1, 0))

    def from_next(v):
        return jnp.where(last_chunk, 0.0, pltpu.roll(v, SUBLANES - 1, 0))

    xpad[halo:halo + rows, :] = x_ref[...]
    xpad[0:SUBLANES, :] = from_prev(x_ref[rows - 2 * SUBLANES:rows - SUBLANES, :])
    xpad[SUBLANES:halo, :] = from_prev(x_ref[rows - SUBLANES:rows, :])
    xpad[halo + rows:halo + rows + SUBLANES, :] = from_next(x_ref[0:SUBLANES, :])

    def gates(c, carry):
        r0 = pl.multiple_of(c * LRU_ROWS, LRU_ROWS)
        xc = cb_ref[...]
        for k in range(CONV_W):
            xc = xc + xpad[pl.ds(r0 + k * SUBLANES, LRU_ROWS), :] * cw_ref[k:k + 1, :]
        xcb = xc.astype(BF16)
        for d, (a_ref, u_ref) in enumerate(((a_f, u_f), (a_b, u_b))):
            for n in range(LRU_CH // LRU_BW):
                sl = slice(n * LRU_BW, (n + 1) * LRU_BW)
                t_r = jnp.tanh(_dot(xcb[:, sl], wrg_ref[d, n]) + brg_ref[d:d + 1, sl])
                t_i = jnp.tanh(_dot(xcb[:, sl], wig_ref[d, n]) + big_ref[d:d + 1, sl])
                half_c = (-0.5 * LRU_C) * _softplus(-lam_ref[d:d + 1, sl])
                log_a = half_c * t_r + half_c
                a = jnp.exp(log_a)
                a_ref[pl.ds(r0, LRU_ROWS), sl] = a
                one_minus_a2 = -jnp.tanh(log_a) * (a * a + 1.0)
                root = jnp.where(one_minus_a2 > 0.0, one_minus_a2 * lax.rsqrt(one_minus_a2), 0.0)
                u_ref[pl.ds(r0, LRU_ROWS), sl] = root * (0.5 * t_i + 0.5) * xc[:, sl]
        return carry

    lax.fori_loop(0, rows // LRU_ROWS, gates, 0)

    block_rows = SCAN_BLOCK * SUBLANES

    def advance(a_ref, u_ref, first_step, order, h, store):
        slab = pl.ds(pl.multiple_of(first_step * SUBLANES, block_rows), block_rows)
        a_all, u_all = a_ref[slab, :], u_ref[slab, :]
        a = [a_all[j * SUBLANES:(j + 1) * SUBLANES] for j in order]
        u = [u_all[j * SUBLANES:(j + 1) * SUBLANES] for j in order]
        a01, u01 = a[1] * a[0], a[1] * u[0] + u[1]
        a23, u23 = a[3] * a[2], a[3] * u[2] + u[3]
        a03, u03 = a23 * a01, a23 * u01 + u23
        h4 = a03 * h + u03
        if store:
            h2 = a01 * h + u01
            states = dict(zip(order, (a[0] * h + u[0], h2, a[2] * h2 + u[2], h4)))
            u_ref[slab, :] = jnp.concatenate([states[j] for j in range(SCAN_BLOCK)], axis=0)
        return h4

    def scan(hf, hb, store):
        ascending, descending = tuple(range(SCAN_BLOCK)), tuple(reversed(range(SCAN_BLOCK)))

        def block(k, carry):
            hf, hb = carry
            hf = advance(a_f, u_f, k * SCAN_BLOCK, ascending, hf, store)
            hb = advance(a_b, u_b, chunk_len - (k + 1) * SCAN_BLOCK, descending, hb, store)
            return hf, hb
        return lax.fori_loop(0, chunk_len // SCAN_BLOCK, block, (hf, hb), unroll=SCAN_UNROLL)

    hf, hb = hf0_ref[...], hb0_ref[...]
    if n_chunks > 1:
        ef, eb = scan(hf, hb, False)
        hf = jnp.where(first_chunk, hf, pltpu.roll(ef, 1, 0))
        hb = jnp.where(last_chunk, hb, pltpu.roll(eb, SUBLANES - 1, 0))
    hf, hb = scan(hf, hb, True)
    if with_state:
        hf_ref[...] = hf
        hb_ref[...] = hb

    def combine(c, carry):
        r = pl.ds(pl.multiple_of(c * LRU_ROWS, LRU_ROWS), LRU_ROWS)
        s_ref[r, :] = ((u_f[r, :] + u_b[r, :]) * gg_ref[r, :]).astype(BF16)
        return carry

    lax.fori_loop(0, rows // LRU_ROWS, combine, 0)


def _rg_lru(grp, l, lx, gg, conv_w, conv_b, w_rg, b_rg, w_ig, b_ig, lam, hf0, hb0, with_state):
    assert grp.n_chunks == 1 or not with_state
    rows = grp.chunk_len * SUBLANES
    nb = LRU_CH // LRU_BW
    tile = pl.BlockSpec((rows, LRU_CH), lambda g, c: (g, c))
    chan = lambda r: pl.BlockSpec((None, r, LRU_CH), lambda g, c: (l, 0, c))
    wblk = pl.BlockSpec((None, 2, nb, LRU_BW, LRU_BW), lambda g, c: (l, 0, c, 0, 0))
    state = pl.BlockSpec((SUBLANES, LRU_CH), lambda g, c: (g, c))
    out_specs = [tile]
    out_shape = [grp.shape_tm(LRU_WIDTH, BF16)]
    if with_state:
        out_specs += [state, state]
        out_shape += [jax.ShapeDtypeStruct((grp.n_pseudo, LRU_WIDTH), F32)] * 2
    scratch = [pltpu.VMEM((rows + (CONV_W - 1) * SUBLANES, LRU_CH), F32)] + [pltpu.VMEM((rows, LRU_CH), F32)] * 4
    return pl.pallas_call(
        functools.partial(_lru_kernel, grp.n_chunks, grp.chunk_len, with_state),
        grid=(grp.lru_groups, LRU_WIDTH // LRU_CH),
        in_specs=[tile, tile, chan(CONV_W), chan(1), wblk, chan(2), wblk, chan(2), chan(2), state, state],
        out_specs=out_specs,
        out_shape=out_shape,
        scratch_shapes=scratch,
        compiler_params=_params(2),
        name="rg_lru",
    )(lx, gg, conv_w, conv_b, w_rg, b_rg, w_ig, b_ig, lam, hf0, hb0)


def _merge_kernel(o_ref, s_ref, gates_ref, x_ref, mod_ref, perm_ref, gpost_ref, wom_ref, wol_ref, wout_ref, y_ref):
    y_mla = _dot(_flat(o_ref[...]), wom_ref[...])
    s = _flat(_seq_major_tile(perm_ref[...], s_ref[...]))
    y_lru = _dot(s, wol_ref[...])
    gates = _flat(gates_ref[...])
    z = gates[:, :D_MODEL] * y_mla + gates[:, D_MODEL:] * y_lru
    y = _dot(z.astype(BF16), wout_ref[...])
    g1 = mod_ref[:, :, 2 * D_MODEL:3 * D_MODEL]
    y_ref[...] = x_ref[...] + g1 * _tiled(_rms(y, gpost_ref[...]))


def _merge(grp, l, o, s, gates, x, mod, perm_t, g_post, w_o_mla, w_o_lru, w_out):
    return pl.pallas_call(
        _merge_kernel,
        grid=grp.grid,
        in_specs=[grp.seq_major(ATTN_OUT), grp.time_major(LRU_WIDTH), grp.seq_major(2 * D_MODEL),
                  grp.seq_major(D_MODEL), grp.per_pseudo(N_MOD * D_MODEL, l), _resident(perm_t.shape),
                  _resident((1, D_MODEL), l), _resident(w_o_mla.shape[1:], l), _resident(w_o_lru.shape[1:], l),
                  _resident(w_out.shape[1:], l)],
        out_specs=grp.seq_major(D_MODEL),
        out_shape=grp.shape3(D_MODEL, F32),
        compiler_params=_params(2),
        name="merge",
    )(o, s, gates, x, mod, perm_t, g_post, w_o_mla, w_o_lru, w_out)


def _ffn_kernel(x_ref, mod_ref, gpre_ref, gpost_ref, w1_ref, w2_ref, y_ref, act_ref):
    x = x_ref[...]
    sh2, sc2, g2 = (mod_ref[:, :, k * D_MODEL:(k + 1) * D_MODEL] for k in (3, 4, 5))
    hb = _flat((_rms(x, gpre_ref[...]) * (1.0 + sc2) + sh2).astype(BF16))
    for c in range(FF_HIDDEN // FFN_CHUNK):
        lo = c * FFN_CHUNK
        g = _dot(hb, w1_ref[:, lo:lo + FFN_CHUNK])
        u = _dot(hb, w1_ref[:, FF_HIDDEN + lo:FF_HIDDEN + lo + FFN_CHUNK])
        act_ref[:, lo:lo + FFN_CHUNK] = (g * _sigmoid(g) * u).astype(BF16)
    y = _dot(act_ref[...], w2_ref[...])
    y_ref[...] = x + g2 * _tiled(_rms(y, gpost_ref[...]))


def _ffn(grp, l, x, mod, g_pre, g_post, w1, w2):
    assert grp.chunk_len % FFN_STEPS == 0
    return pl.pallas_call(
        _ffn_kernel,
        grid=(grp.lru_groups, grp.chunk_len // FFN_STEPS),
        in_specs=[grp.seq_major(D_MODEL, FFN_STEPS), grp.per_pseudo(N_MOD * D_MODEL, l),
                  _resident((1, D_MODEL), l), _resident((1, D_MODEL), l), _resident(w1.shape[1:], l),
                  _resident(w2.shape[1:], l)],
        out_specs=grp.seq_major(D_MODEL, FFN_STEPS),
        out_shape=grp.shape3(D_MODEL, F32),
        scratch_shapes=[pltpu.VMEM((SUBLANES * FFN_STEPS, FF_HIDDEN), BF16)],
        compiler_params=_params(2),
        name="ffn",
    )(x, mod, g_pre, g_post, w1, w2)


def _merge_ffn_kernel(o_ref, s_ref, gates_ref, x_ref, mod_ref, perm_ref, gpost_mix_ref, wom_ref, wol_ref, wout_ref,
                      gpre_ref, gpost_ref, w1_ref, w2_ref, y_ref, act_ref):
    y_mla = _dot(_flat(o_ref[...]), wom_ref[...])
    s = _flat(_seq_major_tile(perm_ref[...], s_ref[...]))
    y_lru = _dot(s, wol_ref[...])
    z = gates_ref[:, :, :D_MODEL] * _tiled(y_mla) + gates_ref[:, :, D_MODEL:] * _tiled(y_lru)
    y = _dot(_flat(z.astype(BF16)), wout_ref[...])
    g1 = mod_ref[:, :, 2 * D_MODEL:3 * D_MODEL]
    x = x_ref[...] + g1 * _tiled(_rms(y, gpost_mix_ref[...]))
    sh2, sc2, g2 = (mod_ref[:, :, k * D_MODEL:(k + 1) * D_MODEL] for k in (3, 4, 5))
    hb = _flat((_rms(x, gpre_ref[...]) * (1.0 + sc2) + sh2).astype(BF16))
    for c in range(FF_HIDDEN // FFN_CHUNK):
        lo = c * FFN_CHUNK
        g = _dot(hb, w1_ref[:, lo:lo + FFN_CHUNK])
        u = _dot(hb, w1_ref[:, FF_HIDDEN + lo:FF_HIDDEN + lo + FFN_CHUNK])
        act_ref[:, lo:lo + FFN_CHUNK] = (g * _sigmoid(g) * u).astype(BF16)
    y2 = _dot(act_ref[...], w2_ref[...])
    y_ref[...] = x + g2 * _tiled(_rms(y2, gpost_ref[...]))


def _merge_ffn(grp, l, o, s, gates, x, mod, perm_t, g_post_mix, w_o_mla, w_o_lru, w_out, g_pre, g_post, w1, w2):
    return pl.pallas_call(
        _merge_ffn_kernel,
        grid=grp.grid,
        in_specs=[grp.seq_major(ATTN_OUT), grp.time_major(LRU_WIDTH), grp.seq_major(2 * D_MODEL),
                  grp.seq_major(D_MODEL), grp.per_pseudo(N_MOD * D_MODEL, l), _resident(perm_t.shape),
                  _resident((1, D_MODEL), l), _resident(w_o_mla.shape[1:], l), _resident(w_o_lru.shape[1:], l),
                  _resident(w_out.shape[1:], l), _resident((1, D_MODEL), l), _resident((1, D_MODEL), l),
                  _resident(w1.shape[1:], l), _resident(w2.shape[1:], l)],
        out_specs=grp.seq_major(D_MODEL),
        out_shape=grp.shape3(D_MODEL, F32),
        scratch_shapes=[pltpu.VMEM((TILE_ROWS, FF_HIDDEN), BF16)],
        compiler_params=pltpu.CompilerParams(dimension_semantics=("arbitrary", "arbitrary"),
                                             vmem_limit_bytes=MERGE_FFN_VMEM_BYTES),
        name="merge_ffn",
    )(o, s, gates, x, mod, perm_t, g_post_mix, w_o_mla, w_o_lru, w_out, g_pre, g_post, w1, w2)


def _rot_partner(w):
    half = QK_ROPE // 2
    return jnp.concatenate([-w[..., half:], w[..., :half]], axis=-1)


def _rope_tables(grp):
    n_tokens = grp.seq_len
    rows = n_tokens // GRID_W
    row = jnp.repeat(jnp.arange(rows, dtype=F32), GRID_W)
    col = jnp.tile(jnp.arange(GRID_W, dtype=F32), rows)
    n_freq = QK_ROPE // 4
    inv = ROPE_THETA ** (-jnp.arange(n_freq, dtype=F32) / n_freq)
    ang = jnp.concatenate([row[:, None] * inv, col[:, None] * inv], axis=-1)
    cos, sin = jnp.cos(ang), jnp.sin(ang)
    cos2, sin2 = jnp.concatenate([cos, cos], -1), jnp.concatenate([sin, sin], -1)
    ktab = jnp.concatenate([cos2, sin2, jnp.zeros((n_tokens, LANES - 2 * QK_ROPE), F32)], -1)
    scale = QK_DIM ** -0.5
    tail = jnp.zeros((n_tokens, HEAD_LANES - QK_DIM), F32)
    qcos = jnp.concatenate([jnp.ones((n_tokens, QK_NOPE), F32), cos2, tail], -1) * scale
    qsin = jnp.concatenate([jnp.zeros((n_tokens, QK_NOPE), F32), sin2, tail], -1) * scale

    def by_pseudo(tab):
        tab = tab.reshape(grp.n_chunks, grp.chunk_len, tab.shape[-1])
        return jnp.tile(tab, (SUBLANES // grp.n_chunks, 1, 1))

    return by_pseudo(qcos), by_pseudo(qsin), by_pseudo(ktab)


def kernel(x_prompt, x_sample, cache_ckv, cache_krope, state_lru_fwd, state_lru_bwd, c, c_ctx, w_mod, b_mod, g_pre_mix, g_post_mix, g_pre_ffn, g_post_ffn, w_in, g_q, w_q_up, g_kv, w_kv_up, w_o_mla, conv_w, conv_b, w_rg, b_rg, w_ig, b_ig, lru_lambda, w_o_lru, w_out, w_ffn_in, w_ffn_out):
    batch, seq = x_prompt.shape[:2]
    dec_batch, dec_seq = x_sample.shape[:2]
    past = cache_ckv.shape[2]

    w_in_r = _w_in_layout(w_in)
    wq = w_q_up.reshape(DEPTH, Q_LORA, N_HEADS, QK_DIM)
    q_tail = jnp.zeros((DEPTH, Q_LORA, N_HEADS, HEAD_LANES - QK_DIM), F32)
    q_main = jnp.concatenate([wq, q_tail], axis=-1)
    q_partner = jnp.concatenate([jnp.zeros_like(wq[..., :QK_NOPE]), _rot_partner(wq[..., QK_NOPE:]), q_tail], axis=-1)
    w_q_r = jnp.concatenate([q_main.reshape(DEPTH, Q_LORA, Q_COLS),
                             q_partner.reshape(DEPTH, Q_LORA, Q_COLS)], axis=-1).astype(BF16)
    w_kv_r = w_kv_up.astype(BF16)
    wo = w_o_mla.reshape(DEPTH, N_HEADS, V_DIM, D_MODEL)
    w_o_mla_b = jnp.concatenate([jnp.zeros((DEPTH, N_HEADS, HEAD_LANES - V_DIM, D_MODEL), F32), wo],
                                axis=2).reshape(DEPTH, ATTN_OUT, D_MODEL).astype(BF16)
    w_o_lru_b, w_out_b = w_o_lru.astype(BF16), w_out.astype(BF16)
    w_rg_b, w_ig_b = (0.5 * w_rg).astype(BF16), (0.5 * w_ig).astype(BF16)
    b_rg, b_ig = 0.5 * b_rg, 0.5 * b_ig
    w1_b, w2_b = w_ffn_in.astype(BF16), w_ffn_out.astype(BF16)
    row = lambda v: v.reshape(DEPTH, 1, -1)
    g_pre_mix, g_post_mix, g_pre_ffn, g_post_ffn = map(row, (g_pre_mix, g_post_mix, g_pre_ffn, g_post_ffn))
    g_q, g_kv, conv_b = row(g_q), row(g_kv), row(conv_b)
    perm = _to_time_major()
    perm_b, perm_t_b = jnp.asarray(perm, BF16), jnp.asarray(perm.T, BF16)

    mod_all = _modulation(jnp.concatenate([c_ctx[None, :], c], axis=0), w_mod, b_mod)

    def run_group(grp, x, mod_rows, layer_inputs):
        x = x.reshape(grp.n_pseudo, grp.chunk_len, D_MODEL)
        rope_tabs = _rope_tables(grp) if grp.latent else None
        flat = lambda a: a.reshape(grp.tokens, a.shape[-1])
        mod = mod_all[:, mod_rows][:, :, None, :]
        per_layer = []
        for l in range(DEPTH):
            q, ckv, kv, kr, lx, gg, gates = _in_proj(grp, l, x, mod, perm_b, g_pre_mix, w_in_r, g_q, w_q_r,
                                                     g_kv, w_kv_r, rope_tabs)
            hf0, hb0, kv_ctx, kr_ctx = layer_inputs(l)
            o = _attention(grp, flat(q), flat(kv), flat(kr), kv_ctx, kr_ctx)
            lru = _rg_lru(grp, l, lx, gg, conv_w, conv_b, w_rg_b, b_rg, w_ig_b, b_ig, lru_lambda,
                          hf0, hb0, not grp.latent)
            o = o.reshape(grp.n_pseudo, grp.chunk_len, ATTN_OUT)
            x = _merge_ffn(grp, l, o, lru[0], gates, x, mod, perm_t_b, g_post_mix, w_o_mla_b, w_o_lru_b, w_out_b,
                           g_pre_ffn, g_post_ffn, w1_b, w2_b)
            per_layer.append((ckv, kr) + tuple(lru[1:]))
        return x, per_layer

    ctx = _Group(batch, seq, latent=False)
    zeros_state = jnp.zeros((batch, LRU_WIDTH), F32)
    y_prompt, ctx_layers = run_group(ctx, x_prompt, np.zeros(ctx.n_pseudo, np.int32),
                                     lambda l: (zeros_state, zeros_state, None, None))
    stack = lambda k, shape: jnp.stack([lay[k].reshape(shape) for lay in ctx_layers], axis=1)
    new_ckv = stack(0, (batch, seq, KV_LORA))
    new_krope = stack(1, (batch, seq, QK_ROPE))
    new_lru_fwd = stack(2, (batch, LRU_WIDTH))
    new_lru_bwd = stack(3, (batch, LRU_WIDTH))

    lat = _Group(dec_batch, dec_seq, latent=True)

    def latent_inputs(l):
        per_chunk = lambda s: jnp.repeat(s[:, l], lat.n_chunks, axis=0)
        kv_ctx = _kv_up(l, cache_ckv[:, l].reshape(dec_batch * past, KV_LORA), w_kv_r, past)
        return (per_chunk(state_lru_fwd), per_chunk(state_lru_bwd), kv_ctx,
                cache_krope[:, l].reshape(dec_batch * past, QK_ROPE))

    y_sample, _ = run_group(lat, x_sample, 1 + np.arange(lat.n_pseudo) // lat.n_chunks, latent_inputs)
    return (y_prompt.reshape(batch, seq, D_MODEL), y_sample.reshape(dec_batch, dec_seq, D_MODEL),
            new_ckv, new_krope, new_lru_fwd, new_lru_bwd)
```
